```python
import math
import jax, jax.numpy as jnp
from jax import lax
import numpy as np

D_MODEL = 1024
BATCH = 2
SEQ = 16384
DEPTH = 2

N_META = 16
MIX_WIDTH = D_MODEL
HEAD_DIM = 64
SB_WIDTH = MIX_WIDTH // 4
SB_HEADS = SB_WIDTH // HEAD_DIM
SB_BLOCK = 128
DN_WIDTH = MIX_WIDTH // 4
DN_HEADS = DN_WIDTH // HEAD_DIM
DN_CONV = 4
DN_CHUNK = 64
S5_WIDTH = MIX_WIDTH - SB_WIDTH - DN_WIDTH
S5_GROUP = 16
S5_GROUPS = S5_WIDTH // S5_GROUP
S5_STATE = 64
D_FF = ((8 * D_MODEL // 3 + 127) // 128) * 128
IN_SPLITS = (SB_WIDTH, SB_WIDTH, SB_WIDTH, 3 * DN_WIDTH, DN_WIDTH, DN_HEADS, DN_HEADS, S5_WIDTH)
IN_WIDTH = sum(IN_SPLITS)
EPS = 1e-6

kernel_name = 'hymba_sb_deltanet_s5_macaron'


def rmsnorm(x, g):
    xf = x.astype(jnp.float32)
    y = xf * lax.rsqrt(jnp.mean(xf * xf, axis=-1, keepdims=True) + EPS)
    return (y * g.astype(jnp.float32)).astype(x.dtype)


def l2norm(x):
    xf = x.astype(jnp.float32)
    return xf * lax.rsqrt(jnp.sum(xf * xf, axis=-1, keepdims=True) + EPS)


def swiglu(x, w_gate, w_up, w_down):
    return (jax.nn.silu(x @ w_gate) * (x @ w_up)) @ w_down


def front_pad(x, n):
    return jnp.pad(x, [(0, 0), (n, 0)] + [(0, 0)] * (x.ndim - 2))


def causal_depthwise_conv(x, w):
    k = w.shape[0]
    return lax.conv_general_dilated(
        x, w[:, None, :], window_strides=(1,), padding=[(k - 1, 0)],
        dimension_numbers=('NWC', 'WIO', 'NWC'), feature_group_count=x.shape[-1])


def stick_breaking_attention(q, k, v):
    bsz, length, nh, hd = q.shape
    pad = (-N_META) % SB_BLOCK
    q, k, v = (jnp.transpose(front_pad(t, pad), (0, 2, 1, 3)) for t in (q, k, v))
    lp = length + pad
    n_blocks = lp // SB_BLOCK
    key_pos = jnp.arange(lp)
    scale = hd ** -0.5

    def block(i):
        q_blk = lax.dynamic_slice_in_dim(q, i * SB_BLOCK, SB_BLOCK, axis=2)
        q_pos = i * SB_BLOCK + jnp.arange(SB_BLOCK)
        z = jnp.einsum('bhqd,bhkd->bhqk', q_blk, k, preferred_element_type=jnp.float32) * scale
        valid = (key_pos[None, :] < q_pos[:, None]) & (key_pos[None, :] >= pad)
        log_keep = jnp.where(valid, jax.nn.log_sigmoid(-z), 0.0)
        after = lax.cumsum(log_keep, axis=3, reverse=True) - log_keep
        w = jnp.where(valid, jnp.exp(jax.nn.log_sigmoid(z) + after), 0.0)
        return jnp.einsum('bhqk,bhkd->bhqd', w.astype(v.dtype), v)

    out = lax.map(block, jnp.arange(n_blocks))
    out = jnp.transpose(out, (1, 0, 3, 2, 4)).reshape(bsz, lp, nh, hd)
    return out[:, pad:]


def chunk_gated_delta_rule(q, k, v, g, beta):
    bsz, nh, length, dk = q.shape
    dv = v.shape[-1]
    c = DN_CHUNK
    n = length // c
    q = q * dk ** -0.5
    qc = q.reshape(bsz, nh, n, c, dk)
    kc = k.reshape(bsz, nh, n, c, dk)
    vc = v.reshape(bsz, nh, n, c, dv)
    bc = beta.reshape(bsz, nh, n, c, 1)
    gc = jnp.cumsum(g.reshape(bsz, nh, n, c), axis=-1)
    incl = jnp.tril(jnp.ones((c, c), dtype=bool))
    strict = jnp.tril(jnp.ones((c, c), dtype=bool), -1)
    decay = jnp.exp(jnp.where(incl, gc[..., :, None] - gc[..., None, :], -jnp.inf))
    kb = kc * bc
    lmat = jnp.where(strict, jnp.einsum('bhnid,bhnjd->bhnij', kb, kc) * decay, 0.0)
    eye = jnp.eye(c, dtype=jnp.float32)
    t_inv = lax.linalg.triangular_solve(lmat + eye, jnp.broadcast_to(eye, lmat.shape),
                                        left_side=True, lower=True, unit_diagonal=True)
    u = jnp.einsum('bhnij,bhnjd->bhnid', t_inv, vc * bc)
    w = jnp.einsum('bhnij,bhnjd->bhnid', t_inv, kb * jnp.exp(gc)[..., None])
    attn = jnp.where(incl, jnp.einsum('bhnid,bhnjd->bhnij', qc, kc) * decay, 0.0)

    def step(state, inp):
        q_i, k_i, u_i, w_i, a_i, g_i = inp
        v_new = u_i - jnp.einsum('bhcd,bhde->bhce', w_i, state)
        o_i = (jnp.einsum('bhcd,bhde->bhce', q_i * jnp.exp(g_i)[..., None], state)
               + jnp.einsum('bhij,bhje->bhie', a_i, v_new))
        g_last = g_i[..., -1:]
        state = (state * jnp.exp(g_last)[..., None]
                 + jnp.einsum('bhcd,bhce->bhde', k_i * jnp.exp(g_last - g_i)[..., None], v_new))
        return state, o_i

    chunks = tuple(jnp.moveaxis(t, 2, 0) for t in (qc, kc, u, w, attn, gc))
    state0 = jnp.zeros((bsz, nh, dk, dv), jnp.float32)
    _, out = lax.scan(step, state0, chunks)
    return jnp.moveaxis(out, 0, 2).reshape(bsz, nh, length, dv)


def gated_deltanet(qkv, z, b, a, conv_w, a_log, dt_bias, out_norm):
    bsz, length, _ = qkv.shape
    qkv = jax.nn.silu(causal_depthwise_conv(qkv, conv_w))
    q, k, v = (t.reshape(bsz, length, DN_HEADS, HEAD_DIM) for t in jnp.split(qkv, 3, axis=-1))
    q, k, v = l2norm(q), l2norm(k), v.astype(jnp.float32)
    beta = jax.nn.sigmoid(b.astype(jnp.float32))
    g = -jnp.exp(a_log.astype(jnp.float32)) * jax.nn.softplus(a.astype(jnp.float32) + dt_bias.astype(jnp.float32))
    pad = (-N_META) % DN_CHUNK
    q, k, v, g, beta = (front_pad(t, pad) for t in (q, k, v, g, beta))
    o = chunk_gated_delta_rule(jnp.moveaxis(q, 2, 1), jnp.moveaxis(k, 2, 1), jnp.moveaxis(v, 2, 1),
                               jnp.moveaxis(g, 2, 1), jnp.moveaxis(beta, 2, 1))
    o = jnp.moveaxis(o, 1, 2)[:, pad:]
    o = rmsnorm(o, out_norm) * jax.nn.silu(z.astype(jnp.float32).reshape(bsz, length, DN_HEADS, HEAD_DIM))
    return o.reshape(bsz, length, DN_WIDTH)


def s5_mixer(u, a_re, a_im, log_dt, b_re, b_im, c_re, c_im, d, w_glu, b_glu):
    bsz, length, _ = u.shape
    f32 = jnp.float32
    uf = u.astype(f32).reshape(bsz, length, S5_GROUPS, S5_GROUP)
    lam = lax.complex(a_re.astype(f32), a_im.astype(f32))
    dt = jnp.exp(log_dt.astype(f32))[:, None]
    log_abar = lam * dt
    abar = jnp.exp(log_abar)
    b_bar = ((abar - 1.0) / lam)[..., None] * lax.complex(b_re.astype(f32), b_im.astype(f32))
    bu = jnp.einsum('blgc,gpc->blgp', uf.astype(jnp.complex64), b_bar)
    steps = jnp.ones((1, length, 1, 1), f32)

    def combine(e1, e2):
        n1, x1 = e1
        n2, x2 = e2
        return n1 + n2, x1 * jnp.exp(n2 * log_abar) + x2

    _, states = lax.associative_scan(combine, (steps, bu), axis=1)
    c_cplx = lax.complex(c_re.astype(f32), c_im.astype(f32))
    y = jnp.real(jnp.einsum('blgp,gcp->blgc', states, c_cplx)) + d.astype(f32).reshape(S5_GROUPS, S5_GROUP) * uf
    y = jax.nn.gelu(y.reshape(bsz, length, S5_WIDTH))
    return y * jax.nn.sigmoid(y @ w_glu.astype(f32) + b_glu.astype(f32))


def hybrid_mixer(h, w_in, sb_out_norm, dn_conv_w, dn_a_log, dn_dt_bias, dn_out_norm,
                 s5_a_re, s5_a_im, s5_log_dt, s5_b_re, s5_b_im, s5_c_re, s5_c_im,
                 s5_d, s5_w_glu, s5_b_glu, s5_out_norm, w_out):
    bsz, length, _ = h.shape
    proj = h @ w_in
    sb_q, sb_k, sb_v, dn_qkv, dn_z, dn_b, dn_a, s5_u = jnp.split(
        proj, np.cumsum(IN_SPLITS)[:-1].tolist(), axis=-1)
    heads = lambda t: t.reshape(bsz, length, SB_HEADS, HEAD_DIM)
    o_sb = stick_breaking_attention(heads(sb_q), heads(sb_k), heads(sb_v))
    o_sb = rmsnorm(o_sb, sb_out_norm).reshape(bsz, length, SB_WIDTH)
    o_dn = gated_deltanet(dn_qkv, dn_z, dn_b, dn_a, dn_conv_w, dn_a_log, dn_dt_bias, dn_out_norm)
    o_s5 = rmsnorm(s5_mixer(s5_u, s5_a_re, s5_a_im, s5_log_dt, s5_b_re, s5_b_im, s5_c_re, s5_c_im,
                            s5_d, s5_w_glu, s5_b_glu), s5_out_norm)
    mixed = jnp.concatenate([o_sb.astype(h.dtype), o_dn.astype(h.dtype), o_s5.astype(h.dtype)], axis=-1)
    return mixed @ w_out


def setup_inputs(seed: int = 0) -> dict:
    key = jax.random.key(seed)
    ks = iter(jax.random.split(key, 40))
    f32 = jnp.float32
    nrm = lambda shape, scale: scale * jax.random.normal(next(ks), shape, f32)
    gain = lambda shape: 1.0 + nrm(shape, 0.02)
    unif = lambda shape, lo, hi: jax.random.uniform(next(ks), shape, f32, minval=lo, maxval=hi)
    dn_dt = jnp.exp(unif((DEPTH, DN_HEADS), math.log(1e-3), math.log(1e-1)))
    return {
        'x': nrm((BATCH, SEQ, D_MODEL), 1.0),
        'meta_tokens': nrm((N_META, D_MODEL), 1.0),
        'ffn1_norm': gain((DEPTH, D_MODEL)),
        'ffn1_w_gate': nrm((DEPTH, D_MODEL, D_FF), D_MODEL ** -0.5),
        'ffn1_w_up': nrm((DEPTH, D_MODEL, D_FF), D_MODEL ** -0.5),
        'ffn1_w_down': nrm((DEPTH, D_FF, D_MODEL), D_FF ** -0.5),
        'mix_norm': gain((DEPTH, D_MODEL)),
        'w_in': nrm((DEPTH, D_MODEL, IN_WIDTH), D_MODEL ** -0.5),
        'sb_out_norm': gain((DEPTH, HEAD_DIM)),
        'dn_conv_w': nrm((DEPTH, DN_CONV, 3 * DN_WIDTH), DN_CONV ** -0.5),
        'dn_a_log': jnp.log(unif((DEPTH, DN_HEADS), 1.0, 16.0)),
        'dn_dt_bias': dn_dt + jnp.log(-jnp.expm1(-dn_dt)),
        'dn_out_norm': gain((DEPTH, HEAD_DIM)),
        's5_a_re': -0.5 + nrm((DEPTH, S5_GROUPS, S5_STATE), 0.01),
        's5_a_im': math.pi * jnp.arange(S5_STATE, dtype=f32) + nrm((DEPTH, S5_GROUPS, S5_STATE), 0.01),
        's5_log_dt': unif((DEPTH, S5_GROUPS), math.log(1e-3), math.log(1e-1)),
        's5_b_re': nrm((DEPTH, S5_GROUPS, S5_STATE, S5_GROUP), (2 * S5_GROUP) ** -0.5),
        's5_b_im': nrm((DEPTH, S5_GROUPS, S5_STATE, S5_GROUP), (2 * S5_GROUP) ** -0.5),
        's5_c_re': nrm((DEPTH, S5_GROUPS, S5_GROUP, S5_STATE), (2 * S5_STATE) ** -0.5),
        's5_c_im': nrm((DEPTH, S5_GROUPS, S5_GROUP, S5_STATE), (2 * S5_STATE) ** -0.5),
        's5_d': nrm((DEPTH, S5_WIDTH), 1.0),
        's5_w_glu': nrm((DEPTH, S5_WIDTH, S5_WIDTH), S5_WIDTH ** -0.5),
        's5_b_glu': nrm((DEPTH, S5_WIDTH), 0.02),
        's5_out_norm': gain((DEPTH, S5_WIDTH)),
        'w_out': nrm((DEPTH, MIX_WIDTH, D_MODEL), MIX_WIDTH ** -0.5),
        'ffn2_norm': gain((DEPTH, D_MODEL)),
        'ffn2_w_gate': nrm((DEPTH, D_MODEL, D_FF), D_MODEL ** -0.5),
        'ffn2_w_up': nrm((DEPTH, D_MODEL, D_FF), D_MODEL ** -0.5),
        'ffn2_w_down': nrm((DEPTH, D_FF, D_MODEL), D_FF ** -0.5),
        'final_norm': gain((D_MODEL,)),
    }


def reference(x, meta_tokens, ffn1_norm, ffn1_w_gate, ffn1_w_up, ffn1_w_down, mix_norm, w_in,
              sb_out_norm, dn_conv_w, dn_a_log, dn_dt_bias, dn_out_norm,
              s5_a_re, s5_a_im, s5_log_dt, s5_b_re, s5_b_im, s5_c_re, s5_c_im,
              s5_d, s5_w_glu, s5_b_glu, s5_out_norm, w_out,
              ffn2_norm, ffn2_w_gate, ffn2_w_up, ffn2_w_down, final_norm):
    bsz = x.shape[0]
    meta = jnp.broadcast_to(meta_tokens[None].astype(x.dtype), (bsz, N_META, D_MODEL))
    h = jnp.concatenate([meta, x], axis=1)
    for l in range(DEPTH):
        h = h + 0.5 * swiglu(rmsnorm(h, ffn1_norm[l]), ffn1_w_gate[l], ffn1_w_up[l], ffn1_w_down[l])
        h = h + hybrid_mixer(rmsnorm(h, mix_norm[l]), w_in[l], sb_out_norm[l], dn_conv_w[l],
                             dn_a_log[l], dn_dt_bias[l], dn_out_norm[l],
                             s5_a_re[l], s5_a_im[l], s5_log_dt[l], s5_b_re[l], s5_b_im[l],
                             s5_c_re[l], s5_c_im[l], s5_d[l], s5_w_glu[l], s5_b_glu[l],
                             s5_out_norm[l], w_out[l])
        h = h + 0.5 * swiglu(rmsnorm(h, ffn2_norm[l]), ffn2_w_gate[l], ffn2_w_up[l], ffn2_w_down[l])
    return rmsnorm(h, final_norm)[:, N_META:]
```

```python
import functools
import math

import jax
import jax.numpy as jnp
from jax import lax
from jax.experimental import pallas as pl
from jax.experimental.pallas import tpu as pltpu

F32 = jnp.float32
BF16 = jnp.bfloat16

N_META = 16
HEAD_DIM = 64
N_HEADS = 4
HW = N_HEADS * HEAD_DIM
SB_BLOCK = 128
DN_CHUNK = 64
DN_CONV = 4
S5_GROUP = 16
S5_STATE = 64
S5_T = 16
S5_GB = 4
EPS = 1e-6
FRONT_PAD = (-N_META) % SB_BLOCK
EXP_UNDERFLOW = -88.0
VMEM_LIMIT = 56 * 1024 * 1024


def _cparams(sem):
    return pltpu.CompilerParams(dimension_semantics=sem, vmem_limit_bytes=VMEM_LIMIT)


def _pick_tile(n, candidates):
    for c in candidates:
        if n % c == 0:
            return c
    raise ValueError(f"no tile for {n}")


def _rms(x):
    return x * lax.rsqrt(jnp.mean(x * x, axis=-1, keepdims=True) + EPS)


def _split_bf16(x):
    hi = x.astype(BF16)
    lo = (x - hi.astype(F32)).astype(BF16)
    return hi, lo


def _dot(a, b):
    return jnp.dot(a, b, preferred_element_type=F32)


def _dot_nt(a, b):
    return lax.dot_general(a, b, (((1,), (1,)), ((), ())), preferred_element_type=F32)


def _dot_tn(a, b):
    return lax.dot_general(a, b, (((0,), (0,)), ((), ())), preferred_element_type=F32)


def _dot_split(x, m_bf16):
    hi, lo = _split_bf16(x)
    return _dot(hi, m_bf16) + _dot(lo, m_bf16)


def _const_spec(shape):
    nd = len(shape)
    return pl.BlockSpec(shape, lambda *_: (0,) * nd, pipeline_mode=pl.Buffered(1))


def _ffn_body(h_ref, g_ref, wg_ref, wu_ref, wd_ref, o_ref, acc_ref):
    x = h_ref[...]
    xn = (_rms(x) * g_ref[...]).astype(BF16)
    acc_ref[...] = jnp.zeros_like(acc_ref)

    def chunk(c, carry):
        gate = _dot(xn, wg_ref[c])
        up = _dot(xn, wu_ref[c])
        act = (gate * jax.nn.sigmoid(gate) * up).astype(BF16)
        acc_ref[...] += _dot(act, wd_ref[c])
        return carry

    lax.fori_loop(0, wg_ref.shape[0], chunk, 0)
    o_ref[...] = x + 0.5 * acc_ref[...]


def _ffn(h, g, wg, wu, wd):
    ntok, d = h.shape
    tm = _pick_tile(ntok, (768, 512, 384, 256, 128))
    nc, _, fc = wg.shape
    return pl.pallas_call(
        _ffn_body,
        grid=(ntok // tm,),
        in_specs=[pl.BlockSpec((tm, d), lambda i: (i, 0)),
                  _const_spec((1, d)),
                  _const_spec((nc, d, fc)), _const_spec((nc, d, fc)), _const_spec((nc, fc, d))],
        out_specs=pl.BlockSpec((tm, d), lambda i: (i, 0)),
        out_shape=jax.ShapeDtypeStruct((ntok, d), F32),
        scratch_shapes=[pltpu.VMEM((tm, d), F32)],
        compiler_params=_cparams(("parallel",)),
        name="ffn",
    )(h, g, wg, wu, wd)


_C_SBQ, _C_SBK, _C_SBV, _C_DNQKV, _C_DNZ, _C_DNBA, _C_S5U, _C_END = 0, 256, 512, 768, 1536, 1792, 1920, 2432


def _inproj_body(h_ref, g_ref, w_ref, q_ref, k_ref, v_ref, dqkv_ref, dz_ref, dba_ref, u_ref):
    xn = (_rms(h_ref[...]) * g_ref[...]).astype(BF16)
    q_ref[...] = (_dot(xn, w_ref[:, _C_SBQ:_C_SBK]) * (HEAD_DIM ** -0.5)).astype(BF16)
    k_ref[...] = _dot(xn, w_ref[:, _C_SBK:_C_SBV]).astype(BF16)
    v_ref[...] = _dot(xn, w_ref[:, _C_SBV:_C_DNQKV]).astype(BF16)
    dqkv_ref[...] = _dot(xn, w_ref[:, _C_DNQKV:_C_DNZ])
    dz_ref[...] = _dot(xn, w_ref[:, _C_DNZ:_C_DNBA])
    dba_ref[...] = _dot(xn, w_ref[:, _C_DNBA:_C_S5U])
    u_ref[...] = _dot(xn, w_ref[:, _C_S5U:_C_END])


def _inproj(h, g, w):
    ntok, d = h.shape
    tm = _pick_tile(ntok, (768, 512, 384, 256, 128))
    widths = (HW, HW, HW, 3 * HW, HW, 128, 512)
    dtypes = (BF16, BF16, BF16, F32, F32, F32, F32)
    return pl.pallas_call(
        _inproj_body,
        grid=(ntok // tm,),
        in_specs=[pl.BlockSpec((tm, d), lambda i: (i, 0)), _const_spec((1, d)), _const_spec(w.shape)],
        out_specs=[pl.BlockSpec((tm, wd), lambda i: (i, 0)) for wd in widths],
        out_shape=[jax.ShapeDtypeStruct((ntok, wd), dt) for wd, dt in zip(widths, dtypes)],
        compiler_params=_cparams(("parallel",)),
        name="inproj",
    )(h, g, w)


def _mixout_body(h_ref, osb_ref, odn_ref, y_ref, wglu_ref, bglu_ref, g5_ref, wout_ref, o_ref):
    y = y_ref[...]
    gate = _dot(y.astype(BF16), wglu_ref[...]) + bglu_ref[...]
    o5 = (_rms(y * jax.nn.sigmoid(gate)) * g5_ref[...]).astype(BF16)
    mixed = (_dot(osb_ref[...], wout_ref[0:HW, :]) + _dot(odn_ref[...], wout_ref[HW:2 * HW, :])
             + _dot(o5, wout_ref[2 * HW:, :]))
    o_ref[...] = h_ref[...] + mixed


def _mixout(h, osb, odn, y5, wglu, bglu, g5, wout):
    ntok, d = h.shape
    tm = _pick_tile(ntok, (768, 512, 384, 256, 128))
    row = lambda w: pl.BlockSpec((tm, w), lambda i: (i, 0))
    return pl.pallas_call(
        _mixout_body,
        grid=(ntok // tm,),
        in_specs=[row(d), row(HW), row(HW), row(512), _const_spec(wglu.shape), _const_spec(bglu.shape),
                  _const_spec(g5.shape), _const_spec(wout.shape)],
        out_specs=row(d),
        out_shape=jax.ShapeDtypeStruct((ntok, d), F32),
        compiler_params=_cparams(("parallel",)),
        name="mixout",
    )(h, osb, odn, y5, wglu, bglu, g5, wout)


def _final_body(h_ref, g_ref, o_ref):
    o_ref[...] = _rms(h_ref[...]) * g_ref[...]


def _final_norm(h, g):
    ntok, d = h.shape
    tm = _pick_tile(ntok, (768, 512, 384, 256, 128))
    return pl.pallas_call(
        _final_body,
        grid=(ntok // tm,),
        in_specs=[pl.BlockSpec((tm, d), lambda i: (i, 0)), _const_spec((1, d))],
        out_specs=pl.BlockSpec((tm, d), lambda i: (i, 0)),
        out_shape=jax.ShapeDtypeStruct((ntok, d), F32),
        compiler_params=_cparams(("parallel",)),
        name="final_norm",
    )(h, g)


def _sb_body(q_ref, k_ref, v_ref, g_ref, o_ref):
    i = pl.program_id(1)
    tq = SB_BLOCK
    q = q_ref[...]
    row = lax.broadcasted_iota(jnp.int32, (tq, tq), 0)
    col = lax.broadcasted_iota(jnp.int32, (tq, tq), 1)
    suffix = jnp.where(row > col, 1.0, 0.0).astype(BF16)
    qpos = i * tq + row

    def cond(state):
        jb, done = state[0], state[1]
        return jnp.logical_and(jb >= 0, jnp.logical_not(done))

    def body(state):
        jb, _, carry, acc = state
        start = pl.multiple_of(jb * tq, tq)
        kpos = jb * tq + col
        valid = jnp.logical_and(kpos < qpos, kpos >= FRONT_PAD)
        new_carry, new_acc = [], []
        top = None
        for h in range(N_HEADS):
            hs = slice(h * HEAD_DIM, (h + 1) * HEAD_DIM)
            kh = k_ref[pl.ds(start, tq), hs]
            vh = v_ref[pl.ds(start, tq), hs]
            z = _dot_nt(q[:, hs], kh)
            sp = jnp.maximum(z, 0.0) + jnp.log(1.0 + jnp.exp(-jnp.abs(z)))
            lk = jnp.where(valid, -sp, 0.0)
            after = _dot_split(lk, suffix)
            w = jnp.where(valid, jnp.exp(z - sp + after + carry[h]), 0.0)
            new_acc.append(acc[h] + _dot(w.astype(BF16), vh))
            c = carry[h] + after[:, 0:1] + lk[:, 0:1]
            new_carry.append(c)
            m = jnp.max(c)
            top = m if top is None else jnp.maximum(top, m)
        return jb - 1, top < EXP_UNDERFLOW, tuple(new_carry), tuple(new_acc)

    zc = tuple(jnp.zeros((tq, 1), F32) for _ in range(N_HEADS))
    za = tuple(jnp.zeros((tq, HEAD_DIM), F32) for _ in range(N_HEADS))
    _, _, _, acc = lax.while_loop(cond, body, (i, jnp.bool_(False), zc, za))
    gain = g_ref[...]
    o_ref[...] = jnp.concatenate([_rms(a) * gain for a in acc], axis=1).astype(o_ref.dtype)


def _sb_attention(q, k, v, gain):
    bsz, lp, _ = q.shape
    nblk = lp // SB_BLOCK
    full = pl.BlockSpec((None, lp, HW), lambda b, i: (b, 0, 0), pipeline_mode=pl.Buffered(1))
    return pl.pallas_call(
        _sb_body,
        grid=(bsz, nblk),
        in_specs=[pl.BlockSpec((None, SB_BLOCK, HW), lambda b, i: (b, i, 0)), full, full,
                  _const_spec((1, HEAD_DIM))],
        out_specs=pl.BlockSpec((None, SB_BLOCK, HW), lambda b, i: (b, i, 0)),
        out_shape=jax.ShapeDtypeStruct((bsz, lp, HW), BF16),
        compiler_params=_cparams(("parallel", "parallel")),
        name="sb_attention",
    )(q, k, v, gain)


def _head_mask(rows, cols, rdiv, cdiv):
    r = lax.broadcasted_iota(jnp.int32, (rows, cols), 0) // rdiv
    c = lax.broadcasted_iota(jnp.int32, (rows, cols), 1) // cdiv
    return r == c


def _dn_body(qkv_ref, halo_ref, z_ref, ba_ref, cw_ref, alog_ref, dtb_ref, gain_ref, o_ref,
             q_s, k_s, kb_s, vb_s, gb_s, state_s, *, tb):
    i = pl.program_id(0)
    nb = qkv_ref.shape[0]
    c = DN_CHUNK

    @pl.when(i == 0)
    def _():
        state_s[...] = jnp.zeros_like(state_s)

    bd_mask = _head_mask(HW, HW, HEAD_DIM, HEAD_DIM)
    ones_bd = jnp.where(bd_mask, 1.0, 0.0).astype(BF16)
    rowi = lax.broadcasted_iota(jnp.int32, (c, HW), 0)
    colj = lax.broadcasted_iota(jnp.int32, (c, HW), 1) % HEAD_DIM
    strict = rowi > colj
    incl = rowi >= colj
    eye = jnp.where(rowi == colj, 1.0, 0.0)
    tri = jnp.where(lax.broadcasted_iota(jnp.int32, (c, c), 0) >= lax.broadcasted_iota(jnp.int32, (c, c), 1),
                    1.0, 0.0)

    def headsum(x):
        return _dot_split(x, ones_bd)

    def blockdiag(y):
        return jnp.where(bd_mask, jnp.concatenate([y] * N_HEADS, axis=0), 0.0)

    def hs_mm(x, y):
        return jnp.dot(x, blockdiag(y), preferred_element_type=F32, precision=lax.Precision.HIGHEST)

    def lanes_of(x, first):
        return jnp.concatenate(
            [jnp.broadcast_to(x[:, first + h:first + h + 1], (tb, HEAD_DIM)) for h in range(N_HEADS)], axis=1)

    cw = cw_ref[...]
    real = (i * tb + lax.broadcasted_iota(jnp.int32, (tb, 1), 0)) >= FRONT_PAD
    for b in range(nb):
        x = qkv_ref[b]
        halo = jnp.where(i > 0, halo_ref[b], 0.0)
        xx = jnp.concatenate([halo, x], axis=0)
        conv = (cw[3:4] * x + cw[2:3] * xx[7:7 + tb] + cw[1:2] * xx[6:6 + tb] + cw[0:1] * xx[5:5 + tb])
        y = conv * jax.nn.sigmoid(conv)
        qr, kr, v = y[:, 0:HW], y[:, HW:2 * HW], y[:, 2 * HW:3 * HW]
        q = qr * lax.rsqrt(headsum(qr * qr) + EPS) * (HEAD_DIM ** -0.5)
        k = kr * lax.rsqrt(headsum(kr * kr) + EPS)
        ba = ba_ref[b]
        beta = jnp.where(real, jax.nn.sigmoid(ba), 0.0)
        gate = jnp.where(real, -jnp.exp(alog_ref[...]) * jax.nn.softplus(ba + dtb_ref[...]), 0.0)
        betab = lanes_of(beta, 0)
        q_s[b] = q
        k_s[b] = k
        kb_s[b] = k * betab
        vb_s[b] = v * betab
        gb_s[b] = lanes_of(gate, N_HEADS)

    gain = gain_ref[...]

    def chunk(ci, carry):
        r = pl.ds(pl.multiple_of(ci * c, c), c)
        for b in range(nb):
            q, k, kb, vb, gb = q_s[b, r, :], k_s[b, r, :], kb_s[b, r, :], vb_s[b, r, :], gb_s[b, r, :]
            gg = jnp.dot(tri, jnp.concatenate([jnp.where(strict, gb, 0.0), gb], axis=1),
                         preferred_element_type=F32, precision=lax.Precision.HIGHEST)
            decay = jnp.exp(gg[:, 0:HW])
            gc = gg[:, HW:]
            kq = _dot_nt(jnp.concatenate([kb, q], axis=0).astype(BF16), blockdiag(k).astype(BF16))
            lmat = jnp.where(strict, kq[0:c] * decay, 0.0)
            attn = jnp.where(incl, kq[c:] * decay, 0.0)
            tinv = eye - lmat
            power = lmat
            for _ in range(5):
                power = hs_mm(power, power)
                tinv = tinv + hs_mm(tinv, power)
            egc = jnp.exp(gc)
            uw = _dot(tinv.astype(BF16),
                      jnp.concatenate([blockdiag(vb), blockdiag(kb * egc)], axis=1).astype(BF16))
            u, w = uw[:, 0:HW], uw[:, HW:]
            state = state_s[b]
            wq = _dot(jnp.concatenate([w, q * egc], axis=0).astype(BF16), state.astype(BF16))
            v_new = u - wq[0:c]
            o = wq[c:] + _dot(attn.astype(BF16), blockdiag(v_new).astype(BF16))
            g_last = gc[c - 1:c, :]
            kg = k * jnp.exp(g_last - gc)
            upd = _dot_tn(kg.astype(BF16), v_new.astype(BF16))
            state_s[b] = state * jnp.exp(g_last) + jnp.where(bd_mask, upd, 0.0)
            zz = z_ref[b, r, :]
            on = o * lax.rsqrt(headsum(o * o) * (1.0 / HEAD_DIM) + EPS) * gain
            o_ref[b, r, :] = (on * (zz * jax.nn.sigmoid(zz))).astype(o_ref.dtype)
        return carry

    lax.fori_loop(0, tb // c, chunk, 0)


def _deltanet(qkv, z, ba, conv_w, alog, dtb, gain):
    bsz, lp, _ = qkv.shape
    tb = _pick_tile(lp, (384, 128))
    halo_blocks = tb // 8
    row = lambda w: pl.BlockSpec((bsz, tb, w), lambda i: (0, i, 0))
    scr = pltpu.VMEM((bsz, tb, HW), F32)
    return pl.pallas_call(
        functools.partial(_dn_body, tb=tb),
        grid=(lp // tb,),
        in_specs=[row(3 * HW),
                  pl.BlockSpec((bsz, 8, 3 * HW), lambda i: (0, jnp.maximum(i * halo_blocks - 1, 0), 0)),
                  row(HW), row(128),
                  _const_spec(conv_w.shape), _const_spec(alog.shape), _const_spec(dtb.shape),
                  _const_spec(gain.shape)],
        out_specs=row(HW),
        out_shape=jax.ShapeDtypeStruct((bsz, lp, HW), BF16),
        scratch_shapes=[scr, scr, scr, scr, scr, pltpu.VMEM((bsz, HW, HW), F32)],
        compiler_params=_cparams(("arbitrary",)),
        name="deltanet",
    )(qkv, qkv, z, ba, conv_w, alog, dtb, gain)


def _s5_body(u_ref, m_ref, wre_ref, wim_ref, v_ref, ar_ref, ai_ref, d_ref, y_ref,
             ere_s, eim_s, sre_s, sim_s):
    ngb, nc, _ = u_ref.shape
    p = S5_STATE
    for g in range(ngb):
        ub = u_ref[g].astype(BF16)
        ere_s[:, g * p:(g + 1) * p] = _dot(ub, wre_ref[g])
        eim_s[:, g * p:(g + 1) * p] = _dot(ub, wim_ref[g])

    ar = ar_ref[0]
    ai = ai_ref[0]

    def step(c, s):
        sr, si = s
        sre_s[pl.ds(c, 1), :] = sr
        sim_s[pl.ds(c, 1), :] = si
        er = ere_s[pl.ds(c, 1), :]
        ei = eim_s[pl.ds(c, 1), :]
        return ar * sr - ai * si + er, ar * si + ai * sr + ei

    zero = jnp.zeros((1, ngb * p), F32)
    lax.fori_loop(0, nc, step, (zero, zero), unroll=8)

    for g in range(ngb):
        u = u_ref[g]
        s_in = jnp.concatenate([sre_s[:, g * p:(g + 1) * p], sim_s[:, g * p:(g + 1) * p]], axis=1)
        y = _dot(u.astype(BF16), m_ref[g]) + _dot(s_in.astype(BF16), v_ref[g]) + d_ref[g] * u
        y_ref[g] = jax.nn.gelu(y)


def _s5(u, mats):
    m, wre, wim, v, ar, ai, d = mats
    bsz, ng, nc, tw = u.shape
    gb = S5_GB
    blk = lambda s: pl.BlockSpec((gb,) + s, lambda b, j: (j,) + (0,) * len(s))
    scr = pltpu.VMEM((nc, gb * S5_STATE), F32)
    return pl.pallas_call(
        _s5_body,
        grid=(bsz, ng // gb),
        in_specs=[pl.BlockSpec((None, gb, nc, tw), lambda b, j: (b, j, 0, 0)),
                  blk((tw, tw)), blk((tw, S5_STATE)), blk((tw, S5_STATE)), blk((2 * S5_STATE, tw)),
                  pl.BlockSpec((1, 1, gb * S5_STATE), lambda b, j: (j, 0, 0)),
                  pl.BlockSpec((1, 1, gb * S5_STATE), lambda b, j: (j, 0, 0)),
                  blk((1, tw))],
        out_specs=pl.BlockSpec((None, gb, nc, tw), lambda b, j: (b, j, 0, 0)),
        out_shape=jax.ShapeDtypeStruct(u.shape, F32),
        scratch_shapes=[scr, scr, scr, scr],
        compiler_params=_cparams(("parallel", "parallel")),
        name="s5",
    )(u, m, wre, wim, v, ar, ai, d)


def _s5_matrices(a_re, a_im, log_dt, b_re, b_im, c_re, c_im, d):
    ng, p = a_re.shape
    t = S5_T
    dt = jnp.exp(log_dt)[:, None]
    lr, li = a_re * dt, a_im * dt
    er = jnp.exp(lr)
    abr, abi = er * jnp.cos(li), er * jnp.sin(li)
    den = a_re * a_re + a_im * a_im
    fr = ((abr - 1.0) * a_re + abi * a_im) / den
    fi = (abi * a_re - (abr - 1.0) * a_im) / den
    bbr = fr[..., None] * b_re - fi[..., None] * b_im
    bbi = fr[..., None] * b_im + fi[..., None] * b_re
    n = jnp.arange(t + 1, dtype=F32)[:, None, None]
    pr = jnp.exp(n * lr[None]) * jnp.cos(n * li[None])
    pi = jnp.exp(n * lr[None]) * jnp.sin(n * li[None])
    car = c_re[None] * pr[:, :, None, :] - c_im[None] * pi[:, :, None, :]
    cai = c_re[None] * pi[:, :, None, :] + c_im[None] * pr[:, :, None, :]
    kern = jnp.einsum('ngop,gpi->ngio', car[:t], bbr) - jnp.einsum('ngop,gpi->ngio', cai[:t], bbi)
    lag = jnp.arange(t)[None, :] - jnp.arange(t)[:, None]
    m = jnp.where((lag >= 0)[None, :, None, :, None],
                  jnp.transpose(kern[jnp.clip(lag, 0, t - 1)], (2, 0, 3, 1, 4)), 0.0)
    m = m.reshape(ng, t * S5_GROUP, t * S5_GROUP)
    wr = pr[:t][::-1][:, :, :, None] * bbr[None] - pi[:t][::-1][:, :, :, None] * bbi[None]
    wi = pr[:t][::-1][:, :, :, None] * bbi[None] + pi[:t][::-1][:, :, :, None] * bbr[None]
    wre = jnp.transpose(wr, (1, 0, 3, 2)).reshape(ng, t * S5_GROUP, p)
    wim = jnp.transpose(wi, (1, 0, 3, 2)).reshape(ng, t * S5_GROUP, p)
    vre = jnp.transpose(car[1:], (1, 3, 0, 2)).reshape(ng, p, t * S5_GROUP)
    vim = jnp.transpose(-cai[1:], (1, 3, 0, 2)).reshape(ng, p, t * S5_GROUP)
    v = jnp.concatenate([vre, vim], axis=1)
    ar = pr[t].reshape(ng // S5_GB, 1, S5_GB * p)
    ai = pi[t].reshape(ng // S5_GB, 1, S5_GB * p)
    dflat = jnp.tile(d.reshape(ng, 1, S5_GROUP), (1, 1, t))
    return m.astype(BF16), wre.astype(BF16), wim.astype(BF16), v.astype(BF16), ar, ai, dflat


def _ffn_weights(w_gate, w_up, w_down, fc=256):
    d, dff = w_gate.shape
    nc = dff // fc
    wg = jnp.transpose(w_gate.reshape(d, nc, fc), (1, 0, 2)).astype(BF16)
    wu = jnp.transpose(w_up.reshape(d, nc, fc), (1, 0, 2)).astype(BF16)
    wd = w_down.reshape(nc, fc, d).astype(BF16)
    return wg, wu, wd


def _inproj_weight(w_in):
    d = w_in.shape[0]
    sb, dn_qkv, dn_z, dn_ba, s5_u = (w_in[:, 0:768], w_in[:, 768:1536], w_in[:, 1536:1792],
                                     w_in[:, 1792:1800], w_in[:, 1800:2312])
    pad = jnp.zeros((d, 128 - 2 * N_HEADS), w_in.dtype)
    return jnp.concatenate([sb, dn_qkv, dn_z, dn_ba, pad, s5_u], axis=1).astype(BF16)


def _lane_vec(x, first, width=128):
    return jnp.zeros((1, width), F32).at[0, first:first + x.shape[0]].set(x.astype(F32))


def kernel(x, meta_tokens, ffn1_norm, ffn1_w_gate, ffn1_w_up, ffn1_w_down, mix_norm, w_in, sb_out_norm, dn_conv_w, dn_a_log, dn_dt_bias, dn_out_norm, s5_a_re, s5_a_im, s5_log_dt, s5_b_re, s5_b_im, s5_c_re, s5_c_im, s5_d, s5_w_glu, s5_b_glu, s5_out_norm, w_out, ffn2_norm, ffn2_w_gate, ffn2_w_up, ffn2_w_down, final_norm):
    bsz, seq, d = x.shape
    depth = w_in.shape[0]
    lp = FRONT_PAD + N_META + seq
    ntok = bsz * lp
    assert lp % SB_BLOCK == 0 and lp % DN_CHUNK == 0 and lp % S5_T == 0
    meta = jnp.broadcast_to(meta_tokens[None].astype(x.dtype), (bsz, N_META, d))
    h = jnp.concatenate([jnp.zeros((bsz, FRONT_PAD, d), x.dtype), meta, x], axis=1).reshape(ntok, d)
    ngroups = s5_a_re.shape[1]
    nchunks = lp // S5_T
    for l in range(depth):
        h = _ffn(h, ffn1_norm[l][None], *_ffn_weights(ffn1_w_gate[l], ffn1_w_up[l], ffn1_w_down[l]))
        sbq, sbk, sbv, dqkv, dz, dba, u5 = _inproj(h, mix_norm[l][None], _inproj_weight(w_in[l]))
        tok3 = lambda a: a.reshape(bsz, lp, a.shape[-1])
        o_sb = _sb_attention(tok3(sbq), tok3(sbk), tok3(sbv), sb_out_norm[l][None])
        o_dn = _deltanet(tok3(dqkv), tok3(dz), tok3(dba), dn_conv_w[l],
                         _lane_vec(dn_a_log[l], N_HEADS), _lane_vec(dn_dt_bias[l], N_HEADS),
                         jnp.tile(dn_out_norm[l], N_HEADS)[None])
        u_chunks = jnp.transpose(u5.reshape(bsz, nchunks, S5_T, ngroups, S5_GROUP), (0, 3, 1, 2, 4))
        y5 = _s5(u_chunks.reshape(bsz, ngroups, nchunks, S5_T * S5_GROUP),
                 _s5_matrices(s5_a_re[l], s5_a_im[l], s5_log_dt[l], s5_b_re[l], s5_b_im[l],
                              s5_c_re[l], s5_c_im[l], s5_d[l]))
        y5 = jnp.transpose(y5.reshape(bsz, ngroups, nchunks, S5_T, S5_GROUP), (0, 2, 3, 1, 4))
        h = _mixout(h, o_sb.reshape(ntok, HW), o_dn.reshape(ntok, HW), y5.reshape(ntok, ngroups * S5_GROUP),
                    s5_w_glu[l].astype(BF16), s5_b_glu[l][None], s5_out_norm[l][None], w_out[l].astype(BF16))
        h = _ffn(h, ffn2_norm[l][None], *_ffn_weights(ffn2_w_gate[l], ffn2_w_up[l], ffn2_w_down[l]))
    out = _final_norm(h, final_norm[None])
    return out.reshape(bsz, lp, d)[:, FRONT_PAD + N_META:]
```

```python
import functools

import jax
import jax.numpy as jnp
from jax import lax
from jax.experimental import pallas as pl
from jax.experimental.pallas import tpu as pltpu

F32 = jnp.float32
BF16 = jnp.bfloat16

N_META = 16
HEAD_DIM = 64
N_HEADS = 4
HW = N_HEADS * HEAD_DIM
SB_BLOCK = 128
SB_WINDOW = 256
DN_CHUNK = 64
DN_CONV = 4
FF_CHUNK = 256
LANES = 128
S5_GROUP = 16
S5_STATE = 64
S5_T = 16
S5_LG = LANES // S5_GROUP
EPS = 1e-6
FRONT_PAD = (-N_META) % SB_BLOCK
EXP_UNDERFLOW = -88.0
VMEM_LIMIT = 56 * 1024 * 1024


def _cparams(sem):
    return pltpu.CompilerParams(dimension_semantics=sem, vmem_limit_bytes=VMEM_LIMIT)


def _pick_tile(n, candidates):
    for c in candidates:
        if n % c == 0:
            return c
    raise ValueError(f"no tile for {n}")


def _rms(x):
    return x * lax.rsqrt(jnp.mean(x * x, axis=-1, keepdims=True) + EPS)


def _split_bf16(x):
    hi = x.astype(BF16)
    lo = (x - hi.astype(F32)).astype(BF16)
    return hi, lo


def _dot(a, b):
    return jnp.dot(a, b, preferred_element_type=F32)


def _dot_nt(a, b):
    return lax.dot_general(a, b, (((1,), (1,)), ((), ())), preferred_element_type=F32)


def _dot_tn(a, b):
    return lax.dot_general(a, b, (((0,), (0,)), ((), ())), preferred_element_type=F32)


def _dot_split(x, m_bf16):
    hi, lo = _split_bf16(x)
    return _dot(hi, m_bf16) + _dot(lo, m_bf16)


def _lockstep(gens):
    results = [None] * len(gens)
    live = list(range(len(gens)))
    while live:
        still = []
        for n in live:
            try:
                next(gens[n])
                still.append(n)
            except StopIteration as stop:
                results[n] = stop.value
        live = still
    return results


def _const_spec(shape):
    nd = len(shape)
    return pl.BlockSpec(shape, lambda *_: (0,) * nd, pipeline_mode=pl.Buffered(1))


def _ffn_body(h_ref, g_ref, wg_ref, wu_ref, wd_ref, o_ref, acc_ref):
    x = h_ref[...]
    xn = (_rms(x) * g_ref[...]).astype(BF16)
    acc_ref[...] = jnp.zeros_like(acc_ref)

    def chunk(c, carry):
        cols = pl.ds(pl.multiple_of(c * FF_CHUNK, FF_CHUNK), FF_CHUNK)
        gate = _dot(xn, wg_ref[:, cols])
        up = _dot(xn, wu_ref[:, cols])
        act = (gate * jax.nn.sigmoid(gate) * up).astype(BF16)
        acc_ref[...] += _dot(act, wd_ref[cols, :])
        return carry

    lax.fori_loop(0, wg_ref.shape[1] // FF_CHUNK, chunk, 0)
    o_ref[...] = x + 0.5 * acc_ref[...]


def _ffn(h, g, wg, wu, wd):
    ntok, d = h.shape
    tm = _pick_tile(ntok, (768, 512, 384, 256, 128))
    return pl.pallas_call(
        _ffn_body,
        grid=(ntok // tm,),
        in_specs=[pl.BlockSpec((tm, d), lambda i: (i, 0)),
                  _const_spec((1, d)),
                  _const_spec(wg.shape), _const_spec(wu.shape), _const_spec(wd.shape)],
        out_specs=pl.BlockSpec((tm, d), lambda i: (i, 0)),
        out_shape=jax.ShapeDtypeStruct((ntok, d), F32),
        scratch_shapes=[pltpu.VMEM((tm, d), F32)],
        compiler_params=_cparams(("parallel",)),
        name="ffn",
    )(h, g, wg, wu, wd)


_C_SBQ, _C_SBK, _C_SBV, _C_DNQKV, _C_DNZ, _C_DNBA, _C_S5U, _C_END = 0, 256, 512, 768, 1536, 1792, 1920, 2432


def _inproj_body(h_ref, g_ref, w_ref, q_ref, k_ref, v_ref, dqkv_ref, dz_ref, dba_ref, u_ref):
    xn = (_rms(h_ref[...]) * g_ref[...]).astype(BF16)
    q_ref[...] = (_dot(xn, w_ref[:, _C_SBQ:_C_SBK]) * (HEAD_DIM ** -0.5)).astype(BF16)
    k_ref[...] = _dot(xn, w_ref[:, _C_SBK:_C_SBV]).astype(BF16)
    v_ref[...] = _dot(xn, w_ref[:, _C_SBV:_C_DNQKV]).astype(BF16)
    dqkv_ref[...] = _dot(xn, w_ref[:, _C_DNQKV:_C_DNZ])
    dz_ref[...] = _dot(xn, w_ref[:, _C_DNZ:_C_DNBA])
    dba_ref[...] = _dot(xn, w_ref[:, _C_DNBA:_C_S5U])
    u_ref[...] = _dot(xn, w_ref[:, _C_S5U:_C_END])


def _inproj(h, g, w):
    ntok, d = h.shape
    tm = _pick_tile(ntok, (768, 512, 384, 256, 128))
    widths = (HW, HW, HW, 3 * HW, HW, LANES, 512)
    dtypes = (BF16, BF16, BF16, F32, F32, F32, F32)
    return pl.pallas_call(
        _inproj_body,
        grid=(ntok // tm,),
        in_specs=[pl.BlockSpec((tm, d), lambda i: (i, 0)), _const_spec((1, d)), _const_spec(w.shape)],
        out_specs=[pl.BlockSpec((tm, wd), lambda i: (i, 0)) for wd in widths],
        out_shape=[jax.ShapeDtypeStruct((ntok, wd), dt) for wd, dt in zip(widths, dtypes)],
        compiler_params=_cparams(("parallel",)),
        name="inproj",
    )(h, g, w)


def _mixout_body(h_ref, osb_ref, odn_ref, y_ref, wglu_ref, bglu_ref, g5_ref, wout_ref, o_ref):
    y = y_ref[...]
    gate = _dot(y.astype(BF16), wglu_ref[...]) + bglu_ref[...]
    o5 = (_rms(y * jax.nn.sigmoid(gate)) * g5_ref[...]).astype(BF16)
    mixed = (_dot(osb_ref[...], wout_ref[0:HW, :]) + _dot(odn_ref[...], wout_ref[HW:2 * HW, :])
             + _dot(o5, wout_ref[2 * HW:, :]))
    o_ref[...] = h_ref[...] + mixed


def _mixout(h, osb, odn, y5, wglu, bglu, g5, wout):
    ntok, d = h.shape
    tm = _pick_tile(ntok, (768, 512, 384, 256, 128))
    row = lambda w: pl.BlockSpec((tm, w), lambda i: (i, 0))
    return pl.pallas_call(
        _mixout_body,
        grid=(ntok // tm,),
        in_specs=[row(d), row(HW), row(HW), row(512), _const_spec(wglu.shape), _const_spec(bglu.shape),
                  _const_spec(g5.shape), _const_spec(wout.shape)],
        out_specs=row(d),
        out_shape=jax.ShapeDtypeStruct((ntok, d), F32),
        compiler_params=_cparams(("parallel",)),
        name="mixout",
    )(h, osb, odn, y5, wglu, bglu, g5, wout)


def _final_body(h_ref, g_ref, o_ref):
    o_ref[...] = _rms(h_ref[...]) * g_ref[...]


def _final_norm(h, g):
    ntok, d = h.shape
    tm = _pick_tile(ntok, (768, 512, 384, 256, 128))
    return pl.pallas_call(
        _final_body,
        grid=(ntok // tm,),
        in_specs=[pl.BlockSpec((tm, d), lambda i: (i, 0)), _const_spec((1, d))],
        out_specs=pl.BlockSpec((tm, d), lambda i: (i, 0)),
        out_shape=jax.ShapeDtypeStruct((ntok, d), F32),
        compiler_params=_cparams(("parallel",)),
        name="final_norm",
    )(h, g)


def _sb_body(q_ref, k_ref, v_ref, sfx_ref, g_ref, o_ref):
    i = pl.program_id(0)
    nb = q_ref.shape[0]
    tq, tk = SB_BLOCK, SB_WINDOW
    row = lax.broadcasted_iota(jnp.int32, (tq, tk), 0)
    col = lax.broadcasted_iota(jnp.int32, (tq, tk), 1)
    qpos = i * tq + row
    chains = [(b, h) for b in range(nb) for h in range(N_HEADS)]
    qs = [q_ref[b, :, h * HEAD_DIM:(h + 1) * HEAD_DIM] for b, h in chains]
    sfx = sfx_ref[...]

    def cond(state):
        w0, done = state[0], state[1]
        return jnp.logical_and(w0 + tk > 0, jnp.logical_not(done))

    def body(state):
        w0, _, carry, acc = state
        start = pl.multiple_of(jnp.maximum(w0, 0), tq)
        kpos = start + col
        valid = jnp.logical_and(kpos < jnp.minimum(qpos, w0 + tk), kpos >= FRONT_PAD)

        def chain(n, b, h):
            hs = slice(h * HEAD_DIM, (h + 1) * HEAD_DIM)
            z = _dot_nt(qs[n], k_ref[b, pl.ds(start, tk), hs])
            yield
            sp = jnp.maximum(z, 0.0) + jnp.log(1.0 + jnp.exp(-jnp.abs(z)))
            lk = jnp.where(valid, -sp, 0.0)
            sums = _dot_split(lk, sfx)
            yield
            c = carry[n]
            expo = z - sp + sums[:, 0:tk] + jnp.concatenate([c] * (tk // LANES), axis=1)
            w = jnp.where(valid, jnp.exp(expo), 0.0)
            pv = _dot(w.astype(BF16), v_ref[b, pl.ds(start, tk), hs])
            yield
            return c + sums[:, tk:], acc[n] + pv

        results = _lockstep([chain(n, b, h) for n, (b, h) in enumerate(chains)])
        new_carry = tuple(r[0] for r in results)
        top = functools.reduce(jnp.maximum, new_carry)
        return w0 - tk, jnp.max(top) < EXP_UNDERFLOW, new_carry, tuple(r[1] for r in results)

    zc = tuple(jnp.zeros((tq, LANES), F32) for _ in chains)
    za = tuple(jnp.zeros((tq, HEAD_DIM), F32) for _ in chains)
    _, _, _, acc = lax.while_loop(cond, body, ((i + 1) * tq - tk, jnp.bool_(False), zc, za))
    gain = g_ref[...]
    for b in range(nb):
        o_ref[b] = jnp.concatenate([_rms(acc[b * N_HEADS + h]) * gain for h in range(N_HEADS)],
                                   axis=1).astype(o_ref.dtype)


def _sb_attention(q, k, v, gain):
    bsz, lp, _ = q.shape
    nblk = lp // SB_BLOCK
    tk = SB_WINDOW
    r = jnp.arange(tk)
    sfx = jnp.concatenate([(r[:, None] > r[None, :]), jnp.ones((tk, LANES), bool)], axis=1).astype(BF16)
    full = pl.BlockSpec((bsz, lp, HW), lambda i: (0, 0, 0), pipeline_mode=pl.Buffered(1))
    return pl.pallas_call(
        _sb_body,
        grid=(nblk,),
        in_specs=[pl.BlockSpec((bsz, SB_BLOCK, HW), lambda i: (0, i, 0)), full, full,
                  _const_spec(sfx.shape), _const_spec((1, HEAD_DIM))],
        out_specs=pl.BlockSpec((bsz, SB_BLOCK, HW), lambda i: (0, i, 0)),
        out_shape=jax.ShapeDtypeStruct((bsz, lp, HW), BF16),
        compiler_params=_cparams(("parallel",)),
        name="sb_attention",
    )(q, k, v, sfx, gain)


def _head_mask(rows, cols, rdiv, cdiv):
    r = lax.broadcasted_iota(jnp.int32, (rows, cols), 0) // rdiv
    c = lax.broadcasted_iota(jnp.int32, (rows, cols), 1) // cdiv
    return r == c


def _dn_body(qkv_ref, halo_ref, z_ref, ba_ref, cw_ref, alog_ref, dtb_ref, gain_ref, o_ref,
             q_s, k_s, kb_s, vb_s, gs_s, gc_s, u_s, wq_s, a_s, kg_s, o_s, state_s, *, tb):
    i = pl.program_id(0)
    nb = qkv_ref.shape[0]
    c = DN_CHUNK

    @pl.when(i == 0)
    def _():
        state_s[...] = jnp.zeros_like(state_s)

    bd_mask = _head_mask(HW, HW, HEAD_DIM, HEAD_DIM)
    ones_bd = jnp.where(bd_mask, 1.0, 0.0).astype(BF16)
    rowi = lax.broadcasted_iota(jnp.int32, (c, HW), 0)
    colj = lax.broadcasted_iota(jnp.int32, (c, HW), 1) % HEAD_DIM
    strict = rowi > colj
    incl = rowi >= colj
    eye = jnp.where(rowi == colj, 1.0, 0.0)
    strict3 = jnp.concatenate([strict] * 3, axis=1)
    tri = jnp.where(lax.broadcasted_iota(jnp.int32, (c, c), 0) >= lax.broadcasted_iota(jnp.int32, (c, c), 1),
                    1.0, 0.0).astype(BF16)
    rt = lax.broadcasted_iota(jnp.int32, (tb, tb), 0)
    ct = lax.broadcasted_iota(jnp.int32, (tb, tb), 1)
    tri_blk = jnp.where(jnp.logical_and(rt // c == ct // c, rt >= ct), 1.0, 0.0).astype(BF16)

    def headsum(x):
        return _dot_split(x, ones_bd)

    def blockdiag(y):
        return jnp.where(bd_mask, jnp.concatenate([y] * N_HEADS, axis=0), jnp.zeros((), y.dtype))

    def lanes_of(x, first):
        return jnp.concatenate(
            [jnp.broadcast_to(x[:, first + h:first + h + 1], (tb, HEAD_DIM)) for h in range(N_HEADS)], axis=1)

    def sum3(x):
        return x[:, 0:HW] + x[:, HW:2 * HW] + x[:, 2 * HW:3 * HW]

    cw = cw_ref[...]
    real = (i * tb + lax.broadcasted_iota(jnp.int32, (tb, 1), 0)) >= FRONT_PAD
    for b in range(nb):
        x = qkv_ref[b]
        halo = jnp.where(i > 0, halo_ref[b], 0.0)
        xx = jnp.concatenate([halo, x], axis=0)
        conv = (cw[3:4] * x + cw[2:3] * xx[7:7 + tb] + cw[1:2] * xx[6:6 + tb] + cw[0:1] * xx[5:5 + tb])
        y = conv * jax.nn.sigmoid(conv)
        qr, kr, v = y[:, 0:HW], y[:, HW:2 * HW], y[:, 2 * HW:3 * HW]
        q = qr * lax.rsqrt(headsum(qr * qr) + EPS) * (HEAD_DIM ** -0.5)
        k = kr * lax.rsqrt(headsum(kr * kr) + EPS)
        ba = ba_ref[b]
        beta = jnp.where(real, jax.nn.sigmoid(ba), 0.0)
        gate = jnp.where(real, -jnp.exp(alog_ref[...]) * jax.nn.softplus(ba + dtb_ref[...]), 0.0)
        betab = lanes_of(beta, 0)
        gb = lanes_of(gate, N_HEADS)
        g1 = gb.astype(BF16)
        r1 = gb - g1.astype(F32)
        g2 = r1.astype(BF16)
        g3 = (r1 - g2.astype(F32)).astype(BF16)
        gs = jnp.concatenate([g1, g2, g3], axis=1)
        q_s[b] = q
        k_s[b] = k
        kb_s[b] = k * betab
        vb_s[b] = v * betab
        gs_s[b] = gs
        gc_s[b] = sum3(_dot(tri_blk, gs))

    def chunk_matrices(q, k, kb, vb, gc, gs):
        dm = sum3(_dot(tri, jnp.where(strict3, gs, jnp.zeros((), BF16))))
        decay = jnp.exp(dm)
        kq = _dot_nt(jnp.concatenate([kb, q], axis=0).astype(BF16), blockdiag(k.astype(BF16)))
        yield
        lmat = jnp.where(strict, kq[0:c] * decay, 0.0)
        attn = jnp.where(incl, kq[c:] * decay, 0.0)
        neg = -lmat
        ph, plo = _split_bf16(neg)
        res = _dot(jnp.concatenate([ph, plo], axis=0), blockdiag(ph))
        res2 = _dot(ph, blockdiag(plo))
        yield
        power = res[0:c] + res[c:] + res2
        tinv = eye + neg
        for step in range(1, 6):
            ph, plo = _split_bf16(power)
            th, tlo = _split_bf16(tinv)
            bh, bl = blockdiag(ph), blockdiag(plo)
            if step < 5:
                res = _dot(jnp.concatenate([ph, plo, th, tlo], axis=0), bh)
                res2 = _dot(jnp.concatenate([ph, th], axis=0), bl)
                yield
                power = res[0:c] + res[c:2 * c] + res2[0:c]
                tinv = tinv + (res[2 * c:3 * c] + res[3 * c:] + res2[c:])
            else:
                res = _dot(jnp.concatenate([th, tlo], axis=0), bh)
                res2 = _dot(th, bl)
                yield
                tinv = tinv + (res[0:c] + res[c:] + res2)
        egc = jnp.exp(gc)
        uw = _dot(tinv.astype(BF16),
                  jnp.concatenate([blockdiag(vb.astype(BF16)), blockdiag((kb * egc).astype(BF16))], axis=1))
        return (uw[:, 0:HW], jnp.concatenate([uw[:, HW:], q * egc], axis=0).astype(BF16), attn.astype(BF16),
                (k * jnp.exp(gc[c - 1:c, :] - gc)).astype(BF16))

    per_trip = 2 if (tb // c) % 2 == 0 else 1

    def matrices(ti, carry):
        chains = [(b, ti * per_trip + n) for n in range(per_trip) for b in range(nb)]
        rows = [pl.ds(pl.multiple_of(ci * c, c), c) for _, ci in chains]
        loaded = [tuple(s[b, r, :] for s in (q_s, k_s, kb_s, vb_s, gc_s, gs_s)) for (b, _), r in zip(chains, rows)]
        results = _lockstep([chunk_matrices(*vals) for vals in loaded])
        for (b, ci), r, (u, wq, attn, kg) in zip(chains, rows, results):
            u_s[b, r, :] = u
            wq_s[b, pl.ds(pl.multiple_of(ci * 2 * c, 2 * c), 2 * c), :] = wq
            a_s[b, r, :] = attn
            kg_s[b, r, :] = kg
        return carry

    lax.fori_loop(0, tb // (c * per_trip), matrices, 0)

    def recur(ci, carry):
        r = pl.ds(pl.multiple_of(ci * c, c), c)
        r2 = pl.ds(pl.multiple_of(ci * 2 * c, 2 * c), 2 * c)
        loaded = [(state_s[b], wq_s[b, r2, :], u_s[b, r, :], a_s[b, r, :], kg_s[b, r, :],
                   gc_s[b, pl.ds(ci * c + c - 1, 1), :]) for b in range(nb)]

        def chunk_step(state, lhs, u, attn, kg, g_last):
            wq = _dot(lhs, state.astype(BF16))
            yield
            v_new = (u - wq[0:c]).astype(BF16)
            av = _dot(attn, blockdiag(v_new))
            upd = _dot_tn(kg, v_new)
            yield
            return wq[c:] + av, state * jnp.exp(g_last) + jnp.where(bd_mask, upd, 0.0)

        results = _lockstep([chunk_step(*vals) for vals in loaded])
        for b, (o, state) in enumerate(results):
            o_s[b, r, :] = o
            state_s[b] = state
        return carry

    lax.fori_loop(0, tb // c, recur, 0)

    gain = gain_ref[...]
    for b in range(nb):
        o = o_s[b]
        zz = z_ref[b]
        on = o * lax.rsqrt(headsum(o * o) * (1.0 / HEAD_DIM) + EPS) * gain
        o_ref[b] = (on * (zz * jax.nn.sigmoid(zz))).astype(o_ref.dtype)


def _deltanet(qkv, z, ba, conv_w, alog, dtb, gain):
    bsz, lp, _ = qkv.shape
    tb = _pick_tile(lp, (384, 128))
    halo_blocks = tb // 8
    row = lambda w: pl.BlockSpec((bsz, tb, w), lambda i: (0, i, 0))
    f32s = pltpu.VMEM((bsz, tb, HW), F32)
    b16s = pltpu.VMEM((bsz, tb, HW), BF16)
    return pl.pallas_call(
        functools.partial(_dn_body, tb=tb),
        grid=(lp // tb,),
        in_specs=[row(3 * HW),
                  pl.BlockSpec((bsz, 8, 3 * HW), lambda i: (0, jnp.maximum(i * halo_blocks - 1, 0), 0)),
                  row(HW), row(LANES),
                  _const_spec(conv_w.shape), _const_spec(alog.shape), _const_spec(dtb.shape),
                  _const_spec(gain.shape)],
        out_specs=row(HW),
        out_shape=jax.ShapeDtypeStruct((bsz, lp, HW), BF16),
        scratch_shapes=[f32s, f32s, f32s, f32s,
                        pltpu.VMEM((bsz, tb, 3 * HW), BF16), f32s,
                        f32s, pltpu.VMEM((bsz, 2 * tb, HW), BF16), b16s, b16s,
                        f32s, pltpu.VMEM((bsz, HW, HW), F32)],
        compiler_params=_cparams(("arbitrary",)),
        name="deltanet",
    )(qkv, qkv, z, ba, conv_w, alog, dtb, gain)


def _s5_body(u_ref, m_ref, w_ref, v_ref, ar_ref, ai_ref, d_ref, y_ref, e_s, sp_s, st_s):
    n = u_ref.shape[0] // S5_T
    half = S5_LG * S5_STATE

    @pl.when(pl.program_id(2) == 0)
    def _():
        st_s[...] = jnp.zeros_like(st_s)

    xs = [u_ref[pl.ds(t, n, stride=S5_T), :] for t in range(S5_T)]
    xcat = jnp.concatenate(xs, axis=1).astype(BF16)
    e_s[...] = _dot(xcat, w_ref[...])
    ar = ar_ref[...]
    ai = ai_ref[...]

    def step(ci, s):
        sr, si = s
        row = pl.ds(ci, 1)
        sp_s[row, 0:half] = sr
        sp_s[row, half:] = si
        er = e_s[row, 0:half]
        ei = e_s[row, half:]
        return ar * sr - ai * si + er, ar * si + ai * sr + ei

    sr, si = lax.fori_loop(0, n, step, (st_s[0:1, :], st_s[1:2, :]), unroll=4)
    st_s[0:1, :] = sr
    st_s[1:2, :] = si
    ycat = _dot(xcat, m_ref[...]) + _dot(sp_s[...].astype(BF16), v_ref[...])
    d = d_ref[...]
    for t in range(S5_T):
        y_ref[pl.ds(t, n, stride=S5_T), :] = jax.nn.gelu(ycat[:, t * LANES:(t + 1) * LANES] + d * xs[t])


def _s5(u, mats):
    m, w, v, ar, ai, d = mats
    bsz, lp, width = u.shape
    ntile = width // LANES
    rows = _pick_tile(lp, (lp // 2,))
    n = rows // S5_T
    mat = lambda a: pl.BlockSpec((None,) + a.shape[1:], lambda j, b, r: (j,) + (0,) * (a.ndim - 1),
                                 pipeline_mode=pl.Buffered(1))
    return pl.pallas_call(
        _s5_body,
        grid=(ntile, bsz, lp // rows),
        in_specs=[pl.BlockSpec((None, rows, LANES), lambda j, b, r: (b, r, j)),
                  mat(m), mat(w), mat(v), mat(ar), mat(ai), mat(d)],
        out_specs=pl.BlockSpec((None, rows, LANES), lambda j, b, r: (b, r, j)),
        out_shape=jax.ShapeDtypeStruct(u.shape, F32),
        scratch_shapes=[pltpu.VMEM((n, 2 * S5_LG * S5_STATE), F32), pltpu.VMEM((n, 2 * S5_LG * S5_STATE), F32),
                        pltpu.VMEM((8, S5_LG * S5_STATE), F32)],
        compiler_params=_cparams(("arbitrary", "arbitrary", "arbitrary")),
        name="s5",
    )(u, m, w, v, ar, ai, d)


def _s5_matrices(a_re, a_im, log_dt, b_re, b_im, c_re, c_im, d):
    ng, p = a_re.shape
    t, lg, cg = S5_T, S5_LG, S5_GROUP
    nj = ng // lg
    dt = jnp.exp(log_dt)[:, None]
    lr, li = a_re * dt, a_im * dt
    er = jnp.exp(lr)
    abr, abi = er * jnp.cos(li), er * jnp.sin(li)
    den = a_re * a_re + a_im * a_im
    fr = ((abr - 1.0) * a_re + abi * a_im) / den
    fi = (abi * a_re - (abr - 1.0) * a_im) / den
    bbr = fr[..., None] * b_re - fi[..., None] * b_im
    bbi = fr[..., None] * b_im + fi[..., None] * b_re
    n = jnp.arange(t + 1, dtype=F32)[:, None, None]
    pr = jnp.exp(n * lr[None]) * jnp.cos(n * li[None])
    pi = jnp.exp(n * lr[None]) * jnp.sin(n * li[None])
    car = c_re[None] * pr[:, :, None, :] - c_im[None] * pi[:, :, None, :]
    cai = c_re[None] * pi[:, :, None, :] + c_im[None] * pr[:, :, None, :]
    eye = jnp.eye(lg, dtype=F32)
    kern = jnp.einsum('ngop,gpi->ngio', car[:t], bbr) - jnp.einsum('ngop,gpi->ngio', cai[:t], bbi)
    mg = jnp.stack([jnp.concatenate([jnp.zeros((tau,) + kern.shape[1:], F32), kern[:t - tau]], axis=0)
                    for tau in range(t)])
    mg = jnp.transpose(mg.reshape(t, t, nj, lg, cg, cg), (2, 0, 3, 4, 1, 5))
    m = (mg[:, :, :, :, :, None, :] * eye[None, None, :, None, None, :, None]).reshape(nj, t * LANES, t * LANES)
    nrev = (t - 1) - jnp.arange(t, dtype=F32)[:, None, None]
    rev_r = (jnp.exp(nrev * lr[None]) * jnp.cos(nrev * li[None]))[..., None]
    rev_i = (jnp.exp(nrev * lr[None]) * jnp.sin(nrev * li[None]))[..., None]

    def state_cols(x):
        x = jnp.transpose(x.reshape(t, nj, lg, p, cg), (1, 0, 2, 4, 3))
        return (x[:, :, :, :, None, :] * eye[None, None, :, None, :, None]).reshape(nj, t * LANES, lg * p)

    w = jnp.concatenate([state_cols(rev_r * bbr[None] - rev_i * bbi[None]),
                         state_cols(rev_r * bbi[None] + rev_i * bbr[None])], axis=2)

    def state_rows(x):
        x = jnp.transpose(x.reshape(t, nj, lg, cg, p), (1, 2, 4, 0, 3))
        return (x[:, :, :, :, None, :] * eye[None, :, None, None, :, None]).reshape(nj, lg * p, t * LANES)

    v = jnp.concatenate([state_rows(car[1:]), state_rows(-cai[1:])], axis=1)
    ar = pr[t].reshape(nj, 1, lg * p)
    ai = pi[t].reshape(nj, 1, lg * p)
    return m.astype(BF16), w.astype(BF16), v.astype(BF16), ar, ai, d.reshape(nj, 1, LANES)


def _inproj_weight(w_in):
    d = w_in.shape[0]
    sb, dn_qkv, dn_z, dn_ba, s5_u = (w_in[:, 0:768], w_in[:, 768:1536], w_in[:, 1536:1792],
                                     w_in[:, 1792:1800], w_in[:, 1800:2312])
    pad = jnp.zeros((d, LANES - 2 * N_HEADS), w_in.dtype)
    return jnp.concatenate([sb, dn_qkv, dn_z, dn_ba, pad, s5_u], axis=1).astype(BF16)


def _lane_vec(x, first, width=LANES):
    return jnp.zeros((1, width), F32).at[0, first:first + x.shape[0]].set(x.astype(F32))


def kernel(x, meta_tokens, ffn1_norm, ffn1_w_gate, ffn1_w_up, ffn1_w_down, mix_norm, w_in, sb_out_norm, dn_conv_w, dn_a_log, dn_dt_bias, dn_out_norm, s5_a_re, s5_a_im, s5_log_dt, s5_b_re, s5_b_im, s5_c_re, s5_c_im, s5_d, s5_w_glu, s5_b_glu, s5_out_norm, w_out, ffn2_norm, ffn2_w_gate, ffn2_w_up, ffn2_w_down, final_norm):
    bsz, seq, d = x.shape
    depth = w_in.shape[0]
    lp = FRONT_PAD + N_META + seq
    ntok = bsz * lp
    assert lp % SB_BLOCK == 0 and lp % DN_CHUNK == 0 and lp % (2 * S5_T) == 0
    meta = jnp.broadcast_to(meta_tokens[None].astype(x.dtype), (bsz, N_META, d))
    h = jnp.concatenate([jnp.zeros((bsz, FRONT_PAD, d), x.dtype), meta, x], axis=1).reshape(ntok, d)
    tok3 = lambda a: a.reshape(bsz, lp, a.shape[-1])
    for l in range(depth):
        h = _ffn(h, ffn1_norm[l][None], ffn1_w_gate[l].astype(BF16), ffn1_w_up[l].astype(BF16),
                 ffn1_w_down[l].astype(BF16))
        sbq, sbk, sbv, dqkv, dz, dba, u5 = _inproj(h, mix_norm[l][None], _inproj_weight(w_in[l]))
        o_sb = _sb_attention(tok3(sbq), tok3(sbk), tok3(sbv), sb_out_norm[l][None])
        o_dn = _deltanet(tok3(dqkv), tok3(dz), tok3(dba), dn_conv_w[l],
                         _lane_vec(dn_a_log[l], N_HEADS), _lane_vec(dn_dt_bias[l], N_HEADS),
                         jnp.tile(dn_out_norm[l], N_HEADS)[None])
        y5 = _s5(tok3(u5), _s5_matrices(s5_a_re[l], s5_a_im[l], s5_log_dt[l], s5_b_re[l], s5_b_im[l],
                                        s5_c_re[l], s5_c_im[l], s5_d[l]))
        h = _mixout(h, o_sb.reshape(ntok, HW), o_dn.reshape(ntok, HW), y5.reshape(ntok, -1),
                    s5_w_glu[l].astype(BF16), s5_b_glu[l][None], s5_out_norm[l][None], w_out[l].astype(BF16))
        h = _ffn(h, ffn2_norm[l][None], ffn2_w_gate[l].astype(BF16), ffn2_w_up[l].astype(BF16),
                 ffn2_w_down[l].astype(BF16))
    out = _final_norm(h, final_norm[None])
    return out.reshape(bsz, lp, d)[:, FRONT_PAD + N_META:]
```

```python
import functools

import jax
import jax.numpy as jnp
from jax import lax
from jax.experimental import pallas as pl
from jax.experimental.pallas import tpu as pltpu

F32 = jnp.float32
BF16 = jnp.bfloat16

N_META = 16
HEAD_DIM = 64
N_HEADS = 4
HW = N_HEADS * HEAD_DIM
SB_BLOCK = 128
SB_WINDOW = 256
DN_CHUNK = 64
DN_CONV = 4
FF_CHUNK = 256
LANES = 128
S5_GROUP = 16
S5_STATE = 64
S5_T = 16
S5_LG = LANES // S5_GROUP
EPS = 1e-6
FRONT_PAD = (-N_META) % SB_BLOCK
EXP_UNDERFLOW = -88.0
VMEM_LIMIT = 56 * 1024 * 1024


def _cparams(sem):
    return pltpu.CompilerParams(dimension_semantics=sem, vmem_limit_bytes=VMEM_LIMIT)


def _pick_tile(n, candidates):
    for c in candidates:
        if n % c == 0:
            return c
    raise ValueError(f"no tile for {n}")


def _rms(x):
    return x * lax.rsqrt(jnp.mean(x * x, axis=-1, keepdims=True) + EPS)


def _split_bf16(x):
    hi = x.astype(BF16)
    lo = (x - hi.astype(F32)).astype(BF16)
    return hi, lo


def _dot(a, b):
    return jnp.dot(a, b, preferred_element_type=F32)


def _dot_nt(a, b):
    return lax.dot_general(a, b, (((1,), (1,)), ((), ())), preferred_element_type=F32)


def _dot_tn(a, b):
    return lax.dot_general(a, b, (((0,), (0,)), ((), ())), preferred_element_type=F32)


def _dot_split(x, m_bf16):
    hi, lo = _split_bf16(x)
    return _dot(hi, m_bf16) + _dot(lo, m_bf16)


def _lockstep(gens):
    results = [None] * len(gens)
    live = list(range(len(gens)))
    while live:
        still = []
        for n in live:
            try:
                next(gens[n])
                still.append(n)
            except StopIteration as stop:
                results[n] = stop.value
        live = still
    return results


def _const_spec(shape):
    nd = len(shape)
    return pl.BlockSpec(shape, lambda *_: (0,) * nd, pipeline_mode=pl.Buffered(1))


def _swiglu_half_step(x, g_ref, wg_ref, wu_ref, wd_ref, acc_ref):
    xn = (_rms(x) * g_ref[...]).astype(BF16)

    def chunk_out(c):
        cols = pl.ds(pl.multiple_of(c * FF_CHUNK, FF_CHUNK), FF_CHUNK)
        gate = _dot(xn, wg_ref[:, cols])
        up = _dot(xn, wu_ref[:, cols])
        act = (gate * jax.nn.sigmoid(gate) * up).astype(BF16)
        return _dot(act, wd_ref[cols, :])

    acc_ref[...] = chunk_out(0)

    def chunk(c, carry):
        acc_ref[...] += chunk_out(c)
        return carry

    lax.fori_loop(1, wg_ref.shape[1] // FF_CHUNK, chunk, 0)
    return x + 0.5 * acc_ref[...]


_C_SBQ, _C_SBK, _C_SBV, _C_DNQKV, _C_DNZ, _C_DNBA, _C_S5U, _C_END = 0, 256, 512, 768, 1536, 1792, 1920, 2432
_PROJ_WIDTHS = (HW, HW, HW, 3 * HW, HW, LANES, 512)
_PROJ_DTYPES = (BF16, BF16, BF16, F32, F32, F32, F32)


def _ffn_inproj_body(h_ref, g_ref, wg_ref, wu_ref, wd_ref, gm_ref, w_ref,
                     o_ref, q_ref, k_ref, v_ref, dqkv_ref, dz_ref, dba_ref, u_ref, acc_ref):
    h = _swiglu_half_step(h_ref[...], g_ref, wg_ref, wu_ref, wd_ref, acc_ref)
    o_ref[...] = h
    xn = (_rms(h) * gm_ref[...]).astype(BF16)
    q_ref[...] = (_dot(xn, w_ref[:, _C_SBQ:_C_SBK]) * (HEAD_DIM ** -0.5)).astype(BF16)
    k_ref[...] = _dot(xn, w_ref[:, _C_SBK:_C_SBV]).astype(BF16)
    v_ref[...] = _dot(xn, w_ref[:, _C_SBV:_C_DNQKV]).astype(BF16)
    dqkv_ref[...] = _dot(xn, w_ref[:, _C_DNQKV:_C_DNZ])
    dz_ref[...] = _dot(xn, w_ref[:, _C_DNZ:_C_DNBA])
    dba_ref[...] = _dot(xn, w_ref[:, _C_DNBA:_C_S5U])
    u_ref[...] = _dot(xn, w_ref[:, _C_S5U:_C_END])


def _ffn_inproj(h, g, wg, wu, wd, gm, w):
    ntok, d = h.shape
    tm = _pick_tile(ntok, (768, 512, 384, 256, 128))
    row = lambda wd_: pl.BlockSpec((tm, wd_), lambda i: (i, 0))
    return pl.pallas_call(
        _ffn_inproj_body,
        grid=(ntok // tm,),
        in_specs=[row(d), _const_spec((1, d)), _const_spec(wg.shape), _const_spec(wu.shape), _const_spec(wd.shape),
                  _const_spec((1, d)), _const_spec(w.shape)],
        out_specs=[row(d)] + [row(wd_) for wd_ in _PROJ_WIDTHS],
        out_shape=[jax.ShapeDtypeStruct((ntok, d), F32)]
        + [jax.ShapeDtypeStruct((ntok, wd_), dt) for wd_, dt in zip(_PROJ_WIDTHS, _PROJ_DTYPES)],
        scratch_shapes=[pltpu.VMEM((tm, d), F32)],
        compiler_params=_cparams(("parallel",)),
        name="ffn_inproj",
    )(h, g, wg, wu, wd, gm, w)


def _mixout_ffn_body(h_ref, osb_ref, odn_ref, y_ref, wglu_ref, bglu_ref, g5_ref, wout_ref,
                     g_ref, wg_ref, wu_ref, wd_ref, *rest, final):
    if final:
        gf_ref, o_ref, acc_ref = rest
    else:
        o_ref, acc_ref = rest
    y = y_ref[...]
    gate = _dot(y.astype(BF16), wglu_ref[...]) + bglu_ref[...]
    o5 = (_rms(y * jax.nn.sigmoid(gate)) * g5_ref[...]).astype(BF16)
    mixed = (_dot(osb_ref[...], wout_ref[0:HW, :]) + _dot(odn_ref[...], wout_ref[HW:2 * HW, :])
             + _dot(o5, wout_ref[2 * HW:, :]))
    h = _swiglu_half_step(h_ref[...] + mixed, g_ref, wg_ref, wu_ref, wd_ref, acc_ref)
    o_ref[...] = _rms(h) * gf_ref[...] if final else h


def _mixout_ffn(h, osb, odn, y5, wglu, bglu, g5, wout, g, wg, wu, wd, gf=None):
    ntok, d = h.shape
    tm = _pick_tile(ntok, (768, 512, 384, 256, 128))
    row = lambda w: pl.BlockSpec((tm, w), lambda i: (i, 0))
    consts = [wglu, bglu, g5, wout, g, wg, wu, wd] + ([gf] if gf is not None else [])
    return pl.pallas_call(
        functools.partial(_mixout_ffn_body, final=gf is not None),
        grid=(ntok // tm,),
        in_specs=[row(d), row(HW), row(HW), row(512)] + [_const_spec(c.shape) for c in consts],
        out_specs=row(d),
        out_shape=jax.ShapeDtypeStruct((ntok, d), F32),
        scratch_shapes=[pltpu.VMEM((tm, d), F32)],
        compiler_params=_cparams(("parallel",)),
        name="mixout_ffn",
    )(h, osb, odn, y5, *consts)


def _sb_body(q_ref, k_ref, v_ref, sfx_ref, g_ref, o_ref):
    i = pl.program_id(0)
    nb = q_ref.shape[0]
    tq, tk = SB_BLOCK, SB_WINDOW
    row = lax.broadcasted_iota(jnp.int32, (tq, tk), 0)
    col = lax.broadcasted_iota(jnp.int32, (tq, tk), 1)
    qpos = i * tq + row
    chains = [(b, h) for b in range(nb) for h in range(N_HEADS)]
    qs = [q_ref[b, :, h * HEAD_DIM:(h + 1) * HEAD_DIM] for b, h in chains]
    sfx = sfx_ref[...]

    def cond(state):
        w0, done = state[0], state[1]
        return jnp.logical_and(w0 + tk > 0, jnp.logical_not(done))

    def body(state):
        w0, _, carry, acc = state
        start = pl.multiple_of(jnp.maximum(w0, 0), tq)
        kpos = start + col
        valid = jnp.logical_and(kpos < jnp.minimum(qpos, w0 + tk), kpos >= FRONT_PAD)

        def chain(n, b, h):
            hs = slice(h * HEAD_DIM, (h + 1) * HEAD_DIM)
            z = _dot_nt(qs[n], k_ref[b, pl.ds(start, tk), hs])
            yield
            sp = jnp.maximum(z, 0.0) + jnp.log(1.0 + jnp.exp(-jnp.abs(z)))
            lk = jnp.where(valid, -sp, 0.0)
            sums = _dot_split(lk, sfx)
            yield
            c = carry[n]
            expo = z - sp + sums[:, 0:tk] + jnp.concatenate([c] * (tk // LANES), axis=1)
            w = jnp.where(valid, jnp.exp(expo), 0.0)
            pv = _dot(w.astype(BF16), v_ref[b, pl.ds(start, tk), hs])
            yield
            return c + sums[:, tk:], acc[n] + pv

        results = _lockstep([chain(n, b, h) for n, (b, h) in enumerate(chains)])
        new_carry = tuple(r[0] for r in results)
        top = functools.reduce(jnp.maximum, new_carry)
        return w0 - tk, jnp.max(top) < EXP_UNDERFLOW, new_carry, tuple(r[1] for r in results)

    zc = tuple(jnp.zeros((tq, LANES), F32) for _ in chains)
    za = tuple(jnp.zeros((tq, HEAD_DIM), F32) for _ in chains)
    _, _, _, acc = lax.while_loop(cond, body, ((i + 1) * tq - tk, jnp.bool_(False), zc, za))
    gain = g_ref[...]
    for b in range(nb):
        o_ref[b] = jnp.concatenate([_rms(acc[b * N_HEADS + h]) * gain for h in range(N_HEADS)],
                                   axis=1).astype(o_ref.dtype)


def _sb_attention(q, k, v, gain):
    bsz, lp, _ = q.shape
    nblk = lp // SB_BLOCK
    tk = SB_WINDOW
    r = jnp.arange(tk)
    sfx = jnp.concatenate([(r[:, None] > r[None, :]), jnp.ones((tk, LANES), bool)], axis=1).astype(BF16)
    full = pl.BlockSpec((bsz, lp, HW), lambda i: (0, 0, 0), pipeline_mode=pl.Buffered(1))
    return pl.pallas_call(
        _sb_body,
        grid=(nblk,),
        in_specs=[pl.BlockSpec((bsz, SB_BLOCK, HW), lambda i: (0, i, 0)), full, full,
                  _const_spec(sfx.shape), _const_spec((1, HEAD_DIM))],
        out_specs=pl.BlockSpec((bsz, SB_BLOCK, HW), lambda i: (0, i, 0)),
        out_shape=jax.ShapeDtypeStruct((bsz, lp, HW), BF16),
        compiler_params=_cparams(("parallel",)),
        name="sb_attention",
    )(q, k, v, sfx, gain)


def _head_mask(rows, cols, rdiv, cdiv):
    r = lax.broadcasted_iota(jnp.int32, (rows, cols), 0) // rdiv
    c = lax.broadcasted_iota(jnp.int32, (rows, cols), 1) // cdiv
    return r == c


def _dn_body(qkv_ref, halo_ref, z_ref, ba_ref, cw_ref, alog_ref, dtb_ref, gain_ref, o_ref,
             q_s, k_s, kb_s, vb_s, gs_s, gc_s, u_s, wq_s, a_s, kg_s, o_s, state_s, *, tb):
    i = pl.program_id(0)
    nb = qkv_ref.shape[0]
    c = DN_CHUNK

    @pl.when(i == 0)
    def _():
        state_s[...] = jnp.zeros_like(state_s)

    bd_mask = _head_mask(HW, HW, HEAD_DIM, HEAD_DIM)
    ones_bd = jnp.where(bd_mask, 1.0, 0.0).astype(BF16)
    rowi = lax.broadcasted_iota(jnp.int32, (c, HW), 0)
    colj = lax.broadcasted_iota(jnp.int32, (c, HW), 1) % HEAD_DIM
    strict = rowi > colj
    incl = rowi >= colj
    eye = jnp.where(rowi == colj, 1.0, 0.0)
    strict3 = jnp.concatenate([strict] * 3, axis=1)
    tri = jnp.where(lax.broadcasted_iota(jnp.int32, (c, c), 0) >= lax.broadcasted_iota(jnp.int32, (c, c), 1),
                    1.0, 0.0).astype(BF16)
    rt = lax.broadcasted_iota(jnp.int32, (tb, tb), 0)
    ct = lax.broadcasted_iota(jnp.int32, (tb, tb), 1)
    tri_blk = jnp.where(jnp.logical_and(rt // c == ct // c, rt >= ct), 1.0, 0.0).astype(BF16)

    def headsum(x):
        return _dot_split(x, ones_bd)

    def blockdiag(y):
        return jnp.where(bd_mask, jnp.concatenate([y] * N_HEADS, axis=0), jnp.zeros((), y.dtype))

    def lanes_of(x, first):
        return jnp.concatenate(
            [jnp.broadcast_to(x[:, first + h:first + h + 1], (tb, HEAD_DIM)) for h in range(N_HEADS)], axis=1)

    def sum3(x):
        return x[:, 0:HW] + x[:, HW:2 * HW] + x[:, 2 * HW:3 * HW]

    cw = cw_ref[...]
    real = (i * tb + lax.broadcasted_iota(jnp.int32, (tb, 1), 0)) >= FRONT_PAD
    for b in range(nb):
        x = qkv_ref[b]
        halo = jnp.where(i > 0, halo_ref[b], 0.0)
        xx = jnp.concatenate([halo, x], axis=0)
        conv = (cw[3:4] * x + cw[2:3] * xx[7:7 + tb] + cw[1:2] * xx[6:6 + tb] + cw[0:1] * xx[5:5 + tb])
        y = conv * jax.nn.sigmoid(conv)
        qr, kr, v = y[:, 0:HW], y[:, HW:2 * HW], y[:, 2 * HW:3 * HW]
        q = qr * lax.rsqrt(headsum(qr * qr) + EPS) * (HEAD_DIM ** -0.5)
        k = kr * lax.rsqrt(headsum(kr * kr) + EPS)
        ba = ba_ref[b]
        beta = jnp.where(real, jax.nn.sigmoid(ba), 0.0)
        gate = jnp.where(real, -jnp.exp(alog_ref[...]) * jax.nn.softplus(ba + dtb_ref[...]), 0.0)
        betab = lanes_of(beta, 0)
        gb = lanes_of(gate, N_HEADS)
        g1 = gb.astype(BF16)
        r1 = gb - g1.astype(F32)
        g2 = r1.astype(BF16)
        g3 = (r1 - g2.astype(F32)).astype(BF16)
        gs = jnp.concatenate([g1, g2, g3], axis=1)
        q_s[b] = q
        k_s[b] = k
        kb_s[b] = k * betab
        vb_s[b] = v * betab
        gs_s[b] = gs
        gc_s[b] = sum3(_dot(tri_blk, gs))

    def chunk_matrices(q, k, kb, vb, gc, gs):
        dm = sum3(_dot(tri, jnp.where(strict3, gs, jnp.zeros((), BF16))))
        decay = jnp.exp(dm)
        kq = _dot_nt(jnp.concatenate([kb, q], axis=0).astype(BF16), blockdiag(k.astype(BF16)))
        yield
        lmat = jnp.where(strict, kq[0:c] * decay, 0.0)
        attn = jnp.where(incl, kq[c:] * decay, 0.0)
        neg = -lmat
        ph, plo = _split_bf16(neg)
        res = _dot(jnp.concatenate([ph, plo], axis=0), blockdiag(ph))
        res2 = _dot(ph, blockdiag(plo))
        yield
        power = res[0:c] + res[c:] + res2
        tinv = eye + neg
        for step in range(1, 6):
            ph, plo = _split_bf16(power)
            th, tlo = _split_bf16(tinv)
            bh, bl = blockdiag(ph), blockdiag(plo)
            if step < 5:
                res = _dot(jnp.concatenate([ph, plo, th, tlo], axis=0), bh)
                res2 = _dot(jnp.concatenate([ph, th], axis=0), bl)
                yield
                power = res[0:c] + res[c:2 * c] + res2[0:c]
                tinv = tinv + (res[2 * c:3 * c] + res[3 * c:] + res2[c:])
            else:
                res = _dot(jnp.concatenate([th, tlo], axis=0), bh)
                res2 = _dot(th, bl)
                yield
                tinv = tinv + (res[0:c] + res[c:] + res2)
        egc = jnp.exp(gc)
        uw = _dot(tinv.astype(BF16),
                  jnp.concatenate([blockdiag(vb.astype(BF16)), blockdiag((kb * egc).astype(BF16))], axis=1))
        return (uw[:, 0:HW], jnp.concatenate([uw[:, HW:], q * egc], axis=0).astype(BF16), attn.astype(BF16),
                (k * jnp.exp(gc[c - 1:c, :] - gc)).astype(BF16))

    per_trip = 2 if (tb // c) % 2 == 0 else 1

    def matrices(ti, carry):
        chains = [(b, ti * per_trip + n) for n in range(per_trip) for b in range(nb)]
        rows = [pl.ds(pl.multiple_of(ci * c, c), c) for _, ci in chains]
        loaded = [tuple(s[b, r, :] for s in (q_s, k_s, kb_s, vb_s, gc_s, gs_s)) for (b, _), r in zip(chains, rows)]
        results = _lockstep([chunk_matrices(*vals) for vals in loaded])
        for (b, ci), r, (u, wq, attn, kg) in zip(chains, rows, results):
            u_s[b, r, :] = u
            wq_s[b, pl.ds(pl.multiple_of(ci * 2 * c, 2 * c), 2 * c), :] = wq
            a_s[b, r, :] = attn
            kg_s[b, r, :] = kg
        return carry

    lax.fori_loop(0, tb // (c * per_trip), matrices, 0)

    def recur(ci, carry):
        r = pl.ds(pl.multiple_of(ci * c, c), c)
        r2 = pl.ds(pl.multiple_of(ci * 2 * c, 2 * c), 2 * c)
        loaded = [(state_s[b], wq_s[b, r2, :], u_s[b, r, :], a_s[b, r, :], kg_s[b, r, :],
                   gc_s[b, pl.ds(ci * c + c - 1, 1), :]) for b in range(nb)]

        def chunk_step(state, lhs, u, attn, kg, g_last):
            wq = _dot(lhs, state.astype(BF16))
            yield
            v_new = (u - wq[0:c]).astype(BF16)
            av = _dot(attn, blockdiag(v_new))
            upd = _dot_tn(kg, v_new)
            yield
            return wq[c:] + av, state * jnp.exp(g_last) + jnp.where(bd_mask, upd, 0.0)

        results = _lockstep([chunk_step(*vals) for vals in loaded])
        for b, (o, state) in enumerate(results):
            o_s[b, r, :] = o
            state_s[b] = state
        return carry

    lax.fori_loop(0, tb // c, recur, 0)

    gain = gain_ref[...]
    for b in range(nb):
        o = o_s[b]
        zz = z_ref[b]
        on = o * lax.rsqrt(headsum(o * o) * (1.0 / HEAD_DIM) + EPS) * gain
        o_ref[b] = (on * (zz * jax.nn.sigmoid(zz))).astype(o_ref.dtype)


def _deltanet(qkv, z, ba, conv_w, alog, dtb, gain):
    bsz, lp, _ = qkv.shape
    tb = _pick_tile(lp, (384, 128))
    halo_blocks = tb // 8
    row = lambda w: pl.BlockSpec((bsz, tb, w), lambda i: (0, i, 0))
    f32s = pltpu.VMEM((bsz, tb, HW), F32)
    b16s = pltpu.VMEM((bsz, tb, HW), BF16)
    return pl.pallas_call(
        functools.partial(_dn_body, tb=tb),
        grid=(lp // tb,),
        in_specs=[row(3 * HW),
                  pl.BlockSpec((bsz, 8, 3 * HW), lambda i: (0, jnp.maximum(i * halo_blocks - 1, 0), 0)),
                  row(HW), row(LANES),
                  _const_spec(conv_w.shape), _const_spec(alog.shape), _const_spec(dtb.shape),
                  _const_spec(gain.shape)],
        out_specs=row(HW),
        out_shape=jax.ShapeDtypeStruct((bsz, lp, HW), BF16),
        scratch_shapes=[f32s, f32s, f32s, f32s,
                        pltpu.VMEM((bsz, tb, 3 * HW), BF16), f32s,
                        f32s, pltpu.VMEM((bsz, 2 * tb, HW), BF16), b16s, b16s,
                        f32s, pltpu.VMEM((bsz, HW, HW), F32)],
        compiler_params=_cparams(("arbitrary",)),
        name="deltanet",
    )(qkv, qkv, z, ba, conv_w, alog, dtb, gain)


def _s5_body(u_ref, m_ref, w_ref, v_ref, ar_ref, ai_ref, d_ref, y_ref, e_s, sp_s, st_s):
    n = u_ref.shape[0] // S5_T
    half = S5_LG * S5_STATE

    @pl.when(pl.program_id(2) == 0)
    def _():
        st_s[...] = jnp.zeros_like(st_s)

    xs = [u_ref[pl.ds(t, n, stride=S5_T), :] for t in range(S5_T)]
    xcat = jnp.concatenate(xs, axis=1).astype(BF16)
    e_s[...] = _dot(xcat, w_ref[...])
    ar = ar_ref[...]
    ai = ai_ref[...]

    def step(ci, s):
        sr, si = s
        row = pl.ds(ci, 1)
        sp_s[row, 0:half] = sr
        sp_s[row, half:] = si
        er = e_s[row, 0:half]
        ei = e_s[row, half:]
        return ar * sr - ai * si + er, ar * si + ai * sr + ei

    sr, si = lax.fori_loop(0, n, step, (st_s[0:1, :], st_s[1:2, :]), unroll=4)
    st_s[0:1, :] = sr
    st_s[1:2, :] = si
    ycat = _dot(xcat, m_ref[...]) + _dot(sp_s[...].astype(BF16), v_ref[...])
    d = d_ref[...]
    for t in range(S5_T):
        y_ref[pl.ds(t, n, stride=S5_T), :] = jax.nn.gelu(ycat[:, t * LANES:(t + 1) * LANES] + d * xs[t])


def _s5(u, mats):
    m, w, v, ar, ai, d = mats
    bsz, lp, width = u.shape
    ntile = width // LANES
    rows = _pick_tile(lp, (lp // 2,))
    n = rows // S5_T
    mat = lambda a: pl.BlockSpec((None,) + a.shape[1:], lambda j, b, r: (j,) + (0,) * (a.ndim - 1),
                                 pipeline_mode=pl.Buffered(1))
    return pl.pallas_call(
        _s5_body,
        grid=(ntile, bsz, lp // rows),
        in_specs=[pl.BlockSpec((None, rows, LANES), lambda j, b, r: (b, r, j)),
                  mat(m), mat(w), mat(v), mat(ar), mat(ai), mat(d)],
        out_specs=pl.BlockSpec((None, rows, LANES), lambda j, b, r: (b, r, j)),
        out_shape=jax.ShapeDtypeStruct(u.shape, F32),
        scratch_shapes=[pltpu.VMEM((n, 2 * S5_LG * S5_STATE), F32), pltpu.VMEM((n, 2 * S5_LG * S5_STATE), F32),
                        pltpu.VMEM((8, S5_LG * S5_STATE), F32)],
        compiler_params=_cparams(("arbitrary", "arbitrary", "arbitrary")),
        name="s5",
    )(u, m, w, v, ar, ai, d)


def _s5_matrices(a_re, a_im, log_dt, b_re, b_im, c_re, c_im, d):
    ng, p = a_re.shape
    t, lg, cg = S5_T, S5_LG, S5_GROUP
    nj = ng // lg
    dt = jnp.exp(log_dt)[:, None]
    lr, li = a_re * dt, a_im * dt
    er = jnp.exp(lr)
    abr, abi = er * jnp.cos(li), er * jnp.sin(li)
    den = a_re * a_re + a_im * a_im
    fr = ((abr - 1.0) * a_re + abi * a_im) / den
    fi = (abi * a_re - (abr - 1.0) * a_im) / den
    bbr = fr[..., None] * b_re - fi[..., None] * b_im
    bbi = fr[..., None] * b_im + fi[..., None] * b_re
    n = jnp.arange(t + 1, dtype=F32)[:, None, None]
    pr = jnp.exp(n * lr[None]) * jnp.cos(n * li[None])
    pi = jnp.exp(n * lr[None]) * jnp.sin(n * li[None])
    car = c_re[None] * pr[:, :, None, :] - c_im[None] * pi[:, :, None, :]
    cai = c_re[None] * pi[:, :, None, :] + c_im[None] * pr[:, :, None, :]
    kern = jnp.einsum('ngop,gpi->ngio', car[:t], bbr) - jnp.einsum('ngop,gpi->ngio', cai[:t], bbi)
    mg = jnp.stack([jnp.concatenate([jnp.zeros((tau,) + kern.shape[1:], F32), kern[:t - tau]], axis=0)
                    for tau in range(t)])
    mg = jnp.transpose(mg, (2, 0, 3, 1, 4)).reshape(nj, lg, t * cg, t * cg).astype(BF16)
    nrev = (t - 1) - jnp.arange(t, dtype=F32)[:, None, None]
    rev_r = (jnp.exp(nrev * lr[None]) * jnp.cos(nrev * li[None]))[..., None]
    rev_i = (jnp.exp(nrev * lr[None]) * jnp.sin(nrev * li[None]))[..., None]
    wg = jnp.stack([rev_r * bbr[None] - rev_i * bbi[None], rev_r * bbi[None] + rev_i * bbr[None]])
    wg = jnp.transpose(wg, (2, 1, 4, 0, 3)).reshape(nj, lg, t * cg, 2 * p).astype(BF16)
    vg = jnp.stack([car[1:], -cai[1:]])
    vg = jnp.transpose(vg, (2, 0, 4, 1, 3)).reshape(nj, lg, 2 * p, t * cg).astype(BF16)

    def placement(rows, cols, col_key, col_group):
        rr = lax.broadcasted_iota(jnp.int32, (lg, rows, cols), 1)
        cc = lax.broadcasted_iota(jnp.int32, (lg, rows, cols), 2)
        gg = lax.broadcasted_iota(jnp.int32, (lg, rows, cols), 0)
        return jnp.logical_and(rr == col_key(cc), col_group(cc) == gg).astype(BF16)

    p_out = placement(t * cg, t * LANES, lambda c: (c // LANES) * cg + c % cg, lambda c: (c // cg) % lg)
    p_state = placement(2 * p, 2 * lg * p, lambda c: (c // (lg * p)) * p + c % p, lambda c: (c // p) % lg)
    place = lambda x, pm: jnp.einsum('jrkc,rcn->jrkn', x, pm, preferred_element_type=BF16)
    m = jnp.transpose(place(mg, p_out).reshape(nj, lg, t, cg, t * LANES), (0, 2, 1, 3, 4)).reshape(
        nj, t * LANES, t * LANES)
    w = jnp.transpose(place(wg, p_state).reshape(nj, lg, t, cg, 2 * lg * p), (0, 2, 1, 3, 4)).reshape(
        nj, t * LANES, 2 * lg * p)
    v = jnp.transpose(place(vg, p_out).reshape(nj, lg, 2, p, t * LANES), (0, 2, 1, 3, 4)).reshape(
        nj, 2 * lg * p, t * LANES)
    ar = pr[t].reshape(nj, 1, lg * p)
    ai = pi[t].reshape(nj, 1, lg * p)
    return m, w, v, ar, ai, d.reshape(nj, 1, LANES)


def _inproj_weight(w_in):
    d = w_in.shape[0]
    sb, dn_qkv, dn_z, dn_ba, s5_u = (w_in[:, 0:768], w_in[:, 768:1536], w_in[:, 1536:1792],
                                     w_in[:, 1792:1800], w_in[:, 1800:2312])
    pad = jnp.zeros((d, LANES - 2 * N_HEADS), w_in.dtype)
    return jnp.concatenate([sb, dn_qkv, dn_z, dn_ba, pad, s5_u], axis=1).astype(BF16)


def _lane_vec(x, first, width=LANES):
    return jnp.zeros((1, width), F32).at[0, first:first + x.shape[0]].set(x.astype(F32))


def kernel(x, meta_tokens, ffn1_norm, ffn1_w_gate, ffn1_w_up, ffn1_w_down, mix_norm, w_in, sb_out_norm, dn_conv_w, dn_a_log, dn_dt_bias, dn_out_norm, s5_a_re, s5_a_im, s5_log_dt, s5_b_re, s5_b_im, s5_c_re, s5_c_im, s5_d, s5_w_glu, s5_b_glu, s5_out_norm, w_out, ffn2_norm, ffn2_w_gate, ffn2_w_up, ffn2_w_down, final_norm):
    bsz, seq, d = x.shape
    depth = w_in.shape[0]
    lp = FRONT_PAD + N_META + seq
    ntok = bsz * lp
    assert lp % SB_BLOCK == 0 and lp % DN_CHUNK == 0 and lp % (2 * S5_T) == 0
    meta = jnp.broadcast_to(meta_tokens[None].astype(x.dtype), (bsz, N_META, d))
    h = jnp.concatenate([jnp.zeros((bsz, FRONT_PAD, d), x.dtype), meta, x], axis=1).reshape(ntok, d)
    tok3 = lambda a: a.reshape(bsz, lp, a.shape[-1])
    bf = lambda a: a.astype(BF16)
    for l in range(depth):
        h, sbq, sbk, sbv, dqkv, dz, dba, u5 = _ffn_inproj(
            h, ffn1_norm[l][None], bf(ffn1_w_gate[l]), bf(ffn1_w_up[l]), bf(ffn1_w_down[l]),
            mix_norm[l][None], _inproj_weight(w_in[l]))
        o_sb = _sb_attention(tok3(sbq), tok3(sbk), tok3(sbv), sb_out_norm[l][None])
        o_dn = _deltanet(tok3(dqkv), tok3(dz), tok3(dba), dn_conv_w[l],
                         _lane_vec(dn_a_log[l], N_HEADS), _lane_vec(dn_dt_bias[l], N_HEADS),
                         jnp.tile(dn_out_norm[l], N_HEADS)[None])
        y5 = _s5(tok3(u5), _s5_matrices(s5_a_re[l], s5_a_im[l], s5_log_dt[l], s5_b_re[l], s5_b_im[l],
                                        s5_c_re[l], s5_c_im[l], s5_d[l]))
        h = _mixout_ffn(h, o_sb.reshape(ntok, HW), o_dn.reshape(ntok, HW), y5.reshape(ntok, -1),
                        bf(s5_w_glu[l]), s5_b_glu[l][None], s5_out_norm[l][None], bf(w_out[l]),
                        ffn2_norm[l][None], bf(ffn2_w_gate[l]), bf(ffn2_w_up[l]), bf(ffn2_w_down[l]),
                        final_norm[None] if l == depth - 1 else None)
    return h.reshape(bsz, lp, d)[:, FRONT_PAD + N_META:]
```

```python
import functools

import jax
import jax.numpy as jnp
from jax import lax
from jax.experimental import pallas as pl
from jax.experimental.pallas import tpu as pltpu

F32 = jnp.float32
BF16 = jnp.bfloat16

N_META = 16
HEAD_DIM = 64
N_HEADS = 4
HW = N_HEADS * HEAD_DIM
SB_BLOCK = 128
SB_WINDOW = 256
DN_CHUNK = 64
DN_CONV = 4
FF_CHUNK = 256
LANES = 128
S5_GROUP = 16
S5_STATE = 64
S5_T = 16
S5_LG = LANES // S5_GROUP
EPS = 1e-6
FRONT_PAD = (-N_META) % SB_BLOCK
EXP_UNDERFLOW = -88.0
VMEM_LIMIT = 56 * 1024 * 1024


def _cparams(sem):
    return pltpu.CompilerParams(dimension_semantics=sem, vmem_limit_bytes=VMEM_LIMIT)


def _pick_tile(n, candidates):
    for c in candidates:
        if n % c == 0:
            return c
    raise ValueError(f"no tile for {n}")


def _rms(x):
    return x * lax.rsqrt(jnp.mean(x * x, axis=-1, keepdims=True) + EPS)


def _split_bf16(x):
    hi = x.astype(BF16)
    lo = (x - hi.astype(F32)).astype(BF16)
    return hi, lo


def _dot(a, b):
    return jnp.dot(a, b, preferred_element_type=F32)


def _dot_nt(a, b):
    return lax.dot_general(a, b, (((1,), (1,)), ((), ())), preferred_element_type=F32)


def _dot_tn(a, b):
    return lax.dot_general(a, b, (((0,), (0,)), ((), ())), preferred_element_type=F32)


def _dot_split(x, m_bf16):
    hi, lo = _split_bf16(x)
    return _dot(hi, m_bf16) + _dot(lo, m_bf16)


def _lockstep(gens):
    results = [None] * len(gens)
    live = list(range(len(gens)))
    while live:
        still = []
        for n in live:
            try:
                next(gens[n])
                still.append(n)
            except StopIteration as stop:
                results[n] = stop.value
        live = still
    return results


def _const_spec(shape):
    nd = len(shape)
    return pl.BlockSpec(shape, lambda *_: (0,) * nd, pipeline_mode=pl.Buffered(1))


def _swiglu_half_step(x, g_ref, wg_ref, wu_ref, wd_ref, acc_ref):
    xn = (_rms(x) * g_ref[...]).astype(BF16)

    def chunk_out(c):
        cols = pl.ds(pl.multiple_of(c * FF_CHUNK, FF_CHUNK), FF_CHUNK)
        gate = _dot(xn, wg_ref[:, cols])
        up = _dot(xn, wu_ref[:, cols])
        act = (gate * jax.nn.sigmoid(gate) * up).astype(BF16)
        return _dot(act, wd_ref[cols, :])

    acc_ref[...] = chunk_out(0)

    def chunk(c, carry):
        acc_ref[...] += chunk_out(c)
        return carry

    lax.fori_loop(1, wg_ref.shape[1] // FF_CHUNK, chunk, 0)
    return x + 0.5 * acc_ref[...]


_C_SBQ, _C_SBK, _C_SBV, _C_DNQKV, _C_DNZ, _C_DNBA, _C_S5U, _C_END = 0, 256, 512, 768, 1536, 1792, 1920, 2432
_PROJ_WIDTHS = (HW, HW, HW, 3 * HW, HW, LANES, 512)
_PROJ_DTYPES = (BF16, BF16, BF16, F32, F32, F32, F32)


def _ffn_inproj_body(h_ref, g_ref, wg_ref, wu_ref, wd_ref, gm_ref, w_ref,
                     o_ref, q_ref, k_ref, v_ref, dqkv_ref, dz_ref, dba_ref, u_ref, acc_ref):
    h = _swiglu_half_step(h_ref[...], g_ref, wg_ref, wu_ref, wd_ref, acc_ref)
    o_ref[...] = h
    xn = (_rms(h) * gm_ref[...]).astype(BF16)
    q_ref[...] = (_dot(xn, w_ref[:, _C_SBQ:_C_SBK]) * (HEAD_DIM ** -0.5)).astype(BF16)
    k_ref[...] = _dot(xn, w_ref[:, _C_SBK:_C_SBV]).astype(BF16)
    v_ref[...] = _dot(xn, w_ref[:, _C_SBV:_C_DNQKV]).astype(BF16)
    dqkv_ref[...] = _dot(xn, w_ref[:, _C_DNQKV:_C_DNZ])
    dz_ref[...] = _dot(xn, w_ref[:, _C_DNZ:_C_DNBA])
    dba_ref[...] = _dot(xn, w_ref[:, _C_DNBA:_C_S5U])
    u_ref[...] = _dot(xn, w_ref[:, _C_S5U:_C_END])


def _ffn_inproj(h, g, wg, wu, wd, gm, w):
    ntok, d = h.shape
    tm = _pick_tile(ntok, (768, 512, 384, 256, 128))
    row = lambda wd_: pl.BlockSpec((tm, wd_), lambda i: (i, 0))
    return pl.pallas_call(
        _ffn_inproj_body,
        grid=(ntok // tm,),
        in_specs=[row(d), _const_spec((1, d)), _const_spec(wg.shape), _const_spec(wu.shape), _const_spec(wd.shape),
                  _const_spec((1, d)), _const_spec(w.shape)],
        out_specs=[row(d)] + [row(wd_) for wd_ in _PROJ_WIDTHS],
        out_shape=[jax.ShapeDtypeStruct((ntok, d), F32)]
        + [jax.ShapeDtypeStruct((ntok, wd_), dt) for wd_, dt in zip(_PROJ_WIDTHS, _PROJ_DTYPES)],
        scratch_shapes=[pltpu.VMEM((tm, d), F32)],
        compiler_params=_cparams(("parallel",)),
        name="ffn_inproj",
    )(h, g, wg, wu, wd, gm, w)


def _mixout_ffn_body(h_ref, osb_ref, odn_ref, y_ref, wglu_ref, bglu_ref, g5_ref, wout_ref,
                     g_ref, wg_ref, wu_ref, wd_ref, *rest, final):
    if final:
        gf_ref, o_ref, acc_ref = rest
    else:
        o_ref, acc_ref = rest
    y = y_ref[...]
    gate = _dot(y.astype(BF16), wglu_ref[...]) + bglu_ref[...]
    o5 = (_rms(y * jax.nn.sigmoid(gate)) * g5_ref[...]).astype(BF16)
    mixed = (_dot(osb_ref[...], wout_ref[0:HW, :]) + _dot(odn_ref[...], wout_ref[HW:2 * HW, :])
             + _dot(o5, wout_ref[2 * HW:, :]))
    h = _swiglu_half_step(h_ref[...] + mixed, g_ref, wg_ref, wu_ref, wd_ref, acc_ref)
    o_ref[...] = _rms(h) * gf_ref[...] if final else h


def _mixout_ffn(h, osb, odn, y5, wglu, bglu, g5, wout, g, wg, wu, wd):
    ntok, d = h.shape
    tm = _pick_tile(ntok, (768, 512, 384, 256, 128))
    row = lambda w: pl.BlockSpec((tm, w), lambda i: (i, 0))
    consts = [wglu, bglu, g5, wout, g, wg, wu, wd]
    return pl.pallas_call(
        functools.partial(_mixout_ffn_body, final=False),
        grid=(ntok // tm,),
        in_specs=[row(d), row(HW), row(HW), row(512)] + [_const_spec(c.shape) for c in consts],
        out_specs=row(d),
        out_shape=jax.ShapeDtypeStruct((ntok, d), F32),
        scratch_shapes=[pltpu.VMEM((tm, d), F32)],
        compiler_params=_cparams(("parallel",)),
        name="mixout_ffn",
    )(h, osb, odn, y5, *consts)


def _mixout_ffn_final(h, osb, odn, y5, wglu, bglu, g5, wout, g, wg, wu, wd, gf, bsz, skip):
    ntok, d = h.shape
    lp = ntok // bsz
    seq = lp - skip
    tm = _pick_tile(seq, (1024, 512, 256, 128))
    per = seq // tm
    sub = 8
    assert lp % sub == 0 and skip % sub == 0
    row = lambda w: pl.BlockSpec(
        (pl.Element(tm), pl.Element(w)),
        lambda b, i: ((b * (lp // sub) + skip // sub + i * (tm // sub)) * sub, 0))
    consts = [wglu, bglu, g5, wout, g, wg, wu, wd, gf]
    return pl.pallas_call(
        functools.partial(_mixout_ffn_body, final=True),
        grid=(bsz, per),
        in_specs=[row(d), row(HW), row(HW), row(512)] + [_const_spec(c.shape) for c in consts],
        out_specs=pl.BlockSpec((tm, d), lambda b, i: (b * per + i, 0)),
        out_shape=jax.ShapeDtypeStruct((bsz * seq, d), F32),
        scratch_shapes=[pltpu.VMEM((tm, d), F32)],
        compiler_params=_cparams(("parallel", "parallel")),
        name="mixout_ffn_final",
    )(h, osb, odn, y5, *consts)


def _sb_body(q_ref, k_ref, v_ref, sfx_ref, g_ref, o_ref):
    i = pl.program_id(0)
    nb = q_ref.shape[0]
    tq, tk = SB_BLOCK, SB_WINDOW
    row = lax.broadcasted_iota(jnp.int32, (tq, tk), 0)
    col = lax.broadcasted_iota(jnp.int32, (tq, tk), 1)
    qpos = i * tq + row
    chains = [(b, h) for b in range(nb) for h in range(N_HEADS)]
    qs = [q_ref[b, :, h * HEAD_DIM:(h + 1) * HEAD_DIM] for b, h in chains]
    sfx = sfx_ref[...]

    def cond(state):
        w0, done = state[0], state[1]
        return jnp.logical_and(w0 + tk > 0, jnp.logical_not(done))

    def body(state):
        w0, _, carry, acc = state
        start = pl.multiple_of(jnp.maximum(w0, 0), tq)
        kpos = start + col
        valid = jnp.logical_and(kpos < jnp.minimum(qpos, w0 + tk), kpos >= FRONT_PAD)

        def chain(n, b, h):
            hs = slice(h * HEAD_DIM, (h + 1) * HEAD_DIM)
            z = _dot_nt(qs[n], k_ref[b, pl.ds(start, tk), hs])
            yield
            sp = jnp.maximum(z, 0.0) + jnp.log(1.0 + jnp.exp(-jnp.abs(z)))
            lk = jnp.where(valid, -sp, 0.0)
            sums = _dot_split(lk, sfx)
            yield
            c = carry[n]
            expo = z - sp + sums[:, 0:tk] + jnp.concatenate([c] * (tk // LANES), axis=1)
            w = jnp.where(valid, jnp.exp(expo), 0.0)
            pv = _dot(w.astype(BF16), v_ref[b, pl.ds(start, tk), hs])
            yield
            return c + sums[:, tk:], acc[n] + pv

        results = _lockstep([chain(n, b, h) for n, (b, h) in enumerate(chains)])
        new_carry = tuple(r[0] for r in results)
        top = functools.reduce(jnp.maximum, new_carry)
        return w0 - tk, jnp.max(top) < EXP_UNDERFLOW, new_carry, tuple(r[1] for r in results)

    zc = tuple(jnp.zeros((tq, LANES), F32) for _ in chains)
    za = tuple(jnp.zeros((tq, HEAD_DIM), F32) for _ in chains)
    _, _, _, acc = lax.while_loop(cond, body, ((i + 1) * tq - tk, jnp.bool_(False), zc, za))
    gain = g_ref[...]
    for b in range(nb):
        o_ref[b] = jnp.concatenate([_rms(acc[b * N_HEADS + h]) * gain for h in range(N_HEADS)],
                                   axis=1).astype(o_ref.dtype)


def _sb_attention(q, k, v, gain):
    bsz, lp, _ = q.shape
    nblk = lp // SB_BLOCK
    tk = SB_WINDOW
    r = jnp.arange(tk)
    sfx = jnp.concatenate([(r[:, None] > r[None, :]), jnp.ones((tk, LANES), bool)], axis=1).astype(BF16)
    full = pl.BlockSpec((bsz, lp, HW), lambda i: (0, 0, 0), pipeline_mode=pl.Buffered(1))
    return pl.pallas_call(
        _sb_body,
        grid=(nblk,),
        in_specs=[pl.BlockSpec((bsz, SB_BLOCK, HW), lambda i: (0, i, 0)), full, full,
                  _const_spec(sfx.shape), _const_spec((1, HEAD_DIM))],
        out_specs=pl.BlockSpec((bsz, SB_BLOCK, HW), lambda i: (0, i, 0)),
        out_shape=jax.ShapeDtypeStruct((bsz, lp, HW), BF16),
        compiler_params=_cparams(("parallel",)),
        name="sb_attention",
    )(q, k, v, sfx, gain)


def _head_mask(rows, cols, rdiv, cdiv):
    r = lax.broadcasted_iota(jnp.int32, (rows, cols), 0) // rdiv
    c = lax.broadcasted_iota(jnp.int32, (rows, cols), 1) // cdiv
    return r == c


def _dn_body(qkv_ref, halo_ref, z_ref, ba_ref, cw_ref, alog_ref, dtb_ref, gain_ref, o_ref,
             q_s, k_s, kb_s, vb_s, gs_s, gc_s, u_s, wq_s, a_s, kg_s, o_s, state_s, *, tb):
    i = pl.program_id(0)
    nb = qkv_ref.shape[0]
    c = DN_CHUNK

    @pl.when(i == 0)
    def _():
        state_s[...] = jnp.zeros_like(state_s)

    bd_mask = _head_mask(HW, HW, HEAD_DIM, HEAD_DIM)
    ones_bd = jnp.where(bd_mask, 1.0, 0.0).astype(BF16)
    rowi = lax.broadcasted_iota(jnp.int32, (c, HW), 0)
    colj = lax.broadcasted_iota(jnp.int32, (c, HW), 1) % HEAD_DIM
    strict = rowi > colj
    incl = rowi >= colj
    eye = jnp.where(rowi == colj, 1.0, 0.0)
    strict3 = jnp.concatenate([strict] * 3, axis=1)
    tri = jnp.where(lax.broadcasted_iota(jnp.int32, (c, c), 0) >= lax.broadcasted_iota(jnp.int32, (c, c), 1),
                    1.0, 0.0).astype(BF16)
    rt = lax.broadcasted_iota(jnp.int32, (tb, tb), 0)
    ct = lax.broadcasted_iota(jnp.int32, (tb, tb), 1)
    tri_blk = jnp.where(jnp.logical_and(rt // c == ct // c, rt >= ct), 1.0, 0.0).astype(BF16)

    def headsum(x):
        return _dot_split(x, ones_bd)

    def blockdiag(y):
        return jnp.where(bd_mask, jnp.concatenate([y] * N_HEADS, axis=0), jnp.zeros((), y.dtype))

    def lanes_of(x, first):
        return jnp.concatenate(
            [jnp.broadcast_to(x[:, first + h:first + h + 1], (tb, HEAD_DIM)) for h in range(N_HEADS)], axis=1)

    def sum3(x):
        return x[:, 0:HW] + x[:, HW:2 * HW] + x[:, 2 * HW:3 * HW]

    cw = cw_ref[...]
    real = (i * tb + lax.broadcasted_iota(jnp.int32, (tb, 1), 0)) >= FRONT_PAD
    for b in range(nb):
        x = qkv_ref[b]
        halo = jnp.where(i > 0, halo_ref[b], 0.0)
        xx = jnp.concatenate([halo, x], axis=0)
        conv = (cw[3:4] * x + cw[2:3] * xx[7:7 + tb] + cw[1:2] * xx[6:6 + tb] + cw[0:1] * xx[5:5 + tb])
        y = conv * jax.nn.sigmoid(conv)
        qr, kr, v = y[:, 0:HW], y[:, HW:2 * HW], y[:, 2 * HW:3 * HW]
        q = qr * lax.rsqrt(headsum(qr * qr) + EPS) * (HEAD_DIM ** -0.5)
        k = kr * lax.rsqrt(headsum(kr * kr) + EPS)
        ba = ba_ref[b]
        beta = jnp.where(real, jax.nn.sigmoid(ba), 0.0)
        gate = jnp.where(real, -jnp.exp(alog_ref[...]) * jax.nn.softplus(ba + dtb_ref[...]), 0.0)
        betab = lanes_of(beta, 0)
        gb = lanes_of(gate, N_HEADS)
        g1 = gb.astype(BF16)
        r1 = gb - g1.astype(F32)
        g2 = r1.astype(BF16)
        g3 = (r1 - g2.astype(F32)).astype(BF16)
        gs = jnp.concatenate([g1, g2, g3], axis=1)
        q_s[b] = q
        k_s[b] = k
        kb_s[b] = k * betab
        vb_s[b] = v * betab
        gs_s[b] = gs
        gc_s[b] = sum3(_dot(tri_blk, gs))

    def chunk_matrices(q, k, kb, vb, gc, gs):
        dm = sum3(_dot(tri, jnp.where(strict3, gs, jnp.zeros((), BF16))))
        decay = jnp.exp(dm)
        kq = _dot_nt(jnp.concatenate([kb, q], axis=0).astype(BF16), blockdiag(k.astype(BF16)))
        yield
        lmat = jnp.where(strict, kq[0:c] * decay, 0.0)
        attn = jnp.where(incl, kq[c:] * decay, 0.0)
        neg = -lmat
        ph, plo = _split_bf16(neg)
        res = _dot(jnp.concatenate([ph, plo], axis=0), blockdiag(ph))
        res2 = _dot(ph, blockdiag(plo))
        yield
        power = res[0:c] + res[c:] + res2
        tinv = eye + neg
        for step in range(1, 6):
            ph, plo = _split_bf16(power)
            th, tlo = _split_bf16(tinv)
            bh, bl = blockdiag(ph), blockdiag(plo)
            if step < 5:
                res = _dot(jnp.concatenate([ph, plo, th, tlo], axis=0), bh)
                res2 = _dot(jnp.concatenate([ph, th], axis=0), bl)
                yield
                power = res[0:c] + res[c:2 * c] + res2[0:c]
                tinv = tinv + (res[2 * c:3 * c] + res[3 * c:] + res2[c:])
            else:
                res = _dot(jnp.concatenate([th, tlo], axis=0), bh)
                res2 = _dot(th, bl)
                yield
                tinv = tinv + (res[0:c] + res[c:] + res2)
        egc = jnp.exp(gc)
        uw = _dot(tinv.astype(BF16),
                  jnp.concatenate([blockdiag(vb.astype(BF16)), blockdiag((kb * egc).astype(BF16))], axis=1))
        return (uw[:, 0:HW], jnp.concatenate([uw[:, HW:], q * egc], axis=0).astype(BF16), attn.astype(BF16),
                (k * jnp.exp(gc[c - 1:c, :] - gc)).astype(BF16))

    per_trip = 2 if (tb // c) % 2 == 0 else 1

    def matrices(ti, carry):
        chains = [(b, ti * per_trip + n) for n in range(per_trip) for b in range(nb)]
        rows = [pl.ds(pl.multiple_of(ci * c, c), c) for _, ci in chains]
        loaded = [tuple(s[b, r, :] for s in (q_s, k_s, kb_s, vb_s, gc_s, gs_s)) for (b, _), r in zip(chains, rows)]
        results = _lockstep([chunk_matrices(*vals) for vals in loaded])
        for (b, ci), r, (u, wq, attn, kg) in zip(chains, rows, results):
            u_s[b, r, :] = u
            wq_s[b, pl.ds(pl.multiple_of(ci * 2 * c, 2 * c), 2 * c), :] = wq
            a_s[b, r, :] = attn
            kg_s[b, r, :] = kg
        return carry

    lax.fori_loop(0, tb // (c * per_trip), matrices, 0)

    def recur(ci, carry):
        r = pl.ds(pl.multiple_of(ci * c, c), c)
        r2 = pl.ds(pl.multiple_of(ci * 2 * c, 2 * c), 2 * c)
        loaded = [(state_s[b], wq_s[b, r2, :], u_s[b, r, :], a_s[b, r, :], kg_s[b, r, :],
                   gc_s[b, pl.ds(ci * c + c - 1, 1), :]) for b in range(nb)]

        def chunk_step(state, lhs, u, attn, kg, g_last):
            wq = _dot(lhs, state.astype(BF16))
            yield
            v_new = (u - wq[0:c]).astype(BF16)
            av = _dot(attn, blockdiag(v_new))
            upd = _dot_tn(kg, v_new)
            yield
            return wq[c:] + av, state * jnp.exp(g_last) + jnp.where(bd_mask, upd, 0.0)

        results = _lockstep([chunk_step(*vals) for vals in loaded])
        for b, (o, state) in enumerate(results):
            o_s[b, r, :] = o
            state_s[b] = state
        return carry

    lax.fori_loop(0, tb // c, recur, 0)

    gain = gain_ref[...]
    for b in range(nb):
        o = o_s[b]
        zz = z_ref[b]
        on = o * lax.rsqrt(headsum(o * o) * (1.0 / HEAD_DIM) + EPS) * gain
        o_ref[b] = (on * (zz * jax.nn.sigmoid(zz))).astype(o_ref.dtype)


def _deltanet(qkv, z, ba, conv_w, alog, dtb, gain):
    bsz, lp, _ = qkv.shape
    tb = _pick_tile(lp, (384, 128))
    halo_blocks = tb // 8
    row = lambda w: pl.BlockSpec((bsz, tb, w), lambda i: (0, i, 0))
    f32s = pltpu.VMEM((bsz, tb, HW), F32)
    b16s = pltpu.VMEM((bsz, tb, HW), BF16)
    return pl.pallas_call(
        functools.partial(_dn_body, tb=tb),
        grid=(lp // tb,),
        in_specs=[row(3 * HW),
                  pl.BlockSpec((bsz, 8, 3 * HW), lambda i: (0, jnp.maximum(i * halo_blocks - 1, 0), 0)),
                  row(HW), row(LANES),
                  _const_spec(conv_w.shape), _const_spec(alog.shape), _const_spec(dtb.shape),
                  _const_spec(gain.shape)],
        out_specs=row(HW),
        out_shape=jax.ShapeDtypeStruct((bsz, lp, HW), BF16),
        scratch_shapes=[f32s, f32s, f32s, f32s,
                        pltpu.VMEM((bsz, tb, 3 * HW), BF16), f32s,
                        f32s, pltpu.VMEM((bsz, 2 * tb, HW), BF16), b16s, b16s,
                        f32s, pltpu.VMEM((bsz, HW, HW), F32)],
        compiler_params=_cparams(("arbitrary",)),
        name="deltanet",
    )(qkv, qkv, z, ba, conv_w, alog, dtb, gain)


def _s5_body(u_ref, m_ref, w_ref, v_ref, ar_ref, ai_ref, d_ref, y_ref, e_s, sp_s, st_s):
    n = u_ref.shape[0] // S5_T
    half = S5_LG * S5_STATE

    @pl.when(pl.program_id(2) == 0)
    def _():
        st_s[...] = jnp.zeros_like(st_s)

    xs = [u_ref[pl.ds(t, n, stride=S5_T), :] for t in range(S5_T)]
    xcat = jnp.concatenate(xs, axis=1).astype(BF16)
    e_s[...] = _dot(xcat, w_ref[...])
    ar = ar_ref[...]
    ai = ai_ref[...]

    def step(ci, s):
        sr, si = s
        row = pl.ds(ci, 1)
        sp_s[row, 0:half] = sr
        sp_s[row, half:] = si
        er = e_s[row, 0:half]
        ei = e_s[row, half:]
        return ar * sr - ai * si + er, ar * si + ai * sr + ei

    sr, si = lax.fori_loop(0, n, step, (st_s[0:1, :], st_s[1:2, :]), unroll=4)
    st_s[0:1, :] = sr
    st_s[1:2, :] = si
    sp = sp_s[...].astype(BF16)
    d = d_ref[...]
    cb = 2 * LANES
    for k in range(S5_T * LANES // cb):
        cols = slice(k * cb, (k + 1) * cb)
        yk = _dot(xcat[:, 0:(k + 1) * cb], m_ref[0:(k + 1) * cb, cols]) + _dot(sp, v_ref[:, cols])
        for t in range(2 * k, 2 * k + 2):
            off = (t - 2 * k) * LANES
            y_ref[pl.ds(t, n, stride=S5_T), :] = jax.nn.gelu(yk[:, off:off + LANES] + d * xs[t])


def _s5(u, mats):
    m, w, v, ar, ai, d = mats
    bsz, lp, width = u.shape
    ntile = width // LANES
    rows = _pick_tile(lp, (lp // 2,))
    n = rows // S5_T
    mat = lambda a: pl.BlockSpec((None,) + a.shape[1:], lambda j, b, r: (j,) + (0,) * (a.ndim - 1),
                                 pipeline_mode=pl.Buffered(1))
    return pl.pallas_call(
        _s5_body,
        grid=(ntile, bsz, lp // rows),
        in_specs=[pl.BlockSpec((None, rows, LANES), lambda j, b, r: (b, r, j)),
                  mat(m), mat(w), mat(v), mat(ar), mat(ai), mat(d)],
        out_specs=pl.BlockSpec((None, rows, LANES), lambda j, b, r: (b, r, j)),
        out_shape=jax.ShapeDtypeStruct(u.shape, F32),
        scratch_shapes=[pltpu.VMEM((n, 2 * S5_LG * S5_STATE), F32), pltpu.VMEM((n, 2 * S5_LG * S5_STATE), F32),
                        pltpu.VMEM((8, S5_LG * S5_STATE), F32)],
        compiler_params=_cparams(("arbitrary", "arbitrary", "arbitrary")),
        name="s5",
    )(u, m, w, v, ar, ai, d)


def _s5_matrices(a_re, a_im, log_dt, b_re, b_im, c_re, c_im, d):
    ng, p = a_re.shape
    t, lg, cg = S5_T, S5_LG, S5_GROUP
    nj = ng // lg
    dt = jnp.exp(log_dt)[:, None]
    lr, li = a_re * dt, a_im * dt
    er = jnp.exp(lr)
    abr, abi = er * jnp.cos(li), er * jnp.sin(li)
    den = a_re * a_re + a_im * a_im
    fr = ((abr - 1.0) * a_re + abi * a_im) / den
    fi = (abi * a_re - (abr - 1.0) * a_im) / den
    bbr = fr[..., None] * b_re - fi[..., None] * b_im
    bbi = fr[..., None] * b_im + fi[..., None] * b_re
    n = jnp.arange(t + 1, dtype=F32)[:, None, None]
    pr = jnp.exp(n * lr[None]) * jnp.cos(n * li[None])
    pi = jnp.exp(n * lr[None]) * jnp.sin(n * li[None])
    car = c_re[None] * pr[:, :, None, :] - c_im[None] * pi[:, :, None, :]
    cai = c_re[None] * pi[:, :, None, :] + c_im[None] * pr[:, :, None, :]
    kern = jnp.einsum('ngop,gpi->ngio', car[:t], bbr) - jnp.einsum('ngop,gpi->ngio', cai[:t], bbi)
    lag = jnp.arange(t)[None, :, None] - jnp.arange(t)[:, None, None]
    select = (lag == jnp.arange(t)[None, None, :]).astype(F32)
    mg = jnp.einsum('aun,ngio->gaiuo', select, kern, precision=lax.Precision.HIGHEST)
    mg = mg.reshape(nj, lg, t * cg, t * cg).astype(BF16)
    nrev = (t - 1) - jnp.arange(t, dtype=F32)[:, None, None]
    rev_r = (jnp.exp(nrev * lr[None]) * jnp.cos(nrev * li[None]))[..., None]
    rev_i = (jnp.exp(nrev * lr[None]) * jnp.sin(nrev * li[None]))[..., None]
    wg = jnp.stack([rev_r * bbr[None] - rev_i * bbi[None], rev_r * bbi[None] + rev_i * bbr[None]])
    wg = jnp.transpose(wg, (2, 1, 4, 0, 3)).reshape(nj, lg, t * cg, 2 * p).astype(BF16)
    vg = jnp.stack([car[1:], -cai[1:]])
    vg = jnp.transpose(vg, (2, 0, 4, 1, 3)).reshape(nj, lg, 2 * p, t * cg).astype(BF16)

    def placement(rows, cols, col_key, col_group):
        rr = lax.broadcasted_iota(jnp.int32, (lg, rows, cols), 1)
        cc = lax.broadcasted_iota(jnp.int32, (lg, rows, cols), 2)
        gg = lax.broadcasted_iota(jnp.int32, (lg, rows, cols), 0)
        return jnp.logical_and(rr == col_key(cc), col_group(cc) == gg).astype(BF16)

    p_out = placement(t * cg, t * LANES, lambda c: (c // LANES) * cg + c % cg, lambda c: (c // cg) % lg)
    p_state = placement(2 * p, 2 * lg * p, lambda c: (c // (lg * p)) * p + c % p, lambda c: (c // p) % lg)
    place = lambda x, pm: jnp.einsum('jrkc,rcn->jrkn', x, pm, preferred_element_type=BF16)
    m = jnp.transpose(place(mg, p_out).reshape(nj, lg, t, cg, t * LANES), (0, 2, 1, 3, 4)).reshape(
        nj, t * LANES, t * LANES)
    w = jnp.transpose(place(wg, p_state).reshape(nj, lg, t, cg, 2 * lg * p), (0, 2, 1, 3, 4)).reshape(
        nj, t * LANES, 2 * lg * p)
    v = jnp.transpose(place(vg, p_out).reshape(nj, lg, 2, p, t * LANES), (0, 2, 1, 3, 4)).reshape(
        nj, 2 * lg * p, t * LANES)
    ar = pr[t].reshape(nj, 1, lg * p)
    ai = pi[t].reshape(nj, 1, lg * p)
    return m, w, v, ar, ai, d.reshape(nj, 1, LANES)


def _inproj_weight(w_in):
    d = w_in.shape[0]
    sb, dn_qkv, dn_z, dn_ba, s5_u = (w_in[:, 0:768], w_in[:, 768:1536], w_in[:, 1536:1792],
                                     w_in[:, 1792:1800], w_in[:, 1800:2312])
    pad = jnp.zeros((d, LANES - 2 * N_HEADS), w_in.dtype)
    return jnp.concatenate([sb, dn_qkv, dn_z, dn_ba, pad, s5_u], axis=1).astype(BF16)


def _lane_vec(x, first, width=LANES):
    return jnp.zeros((1, width), F32).at[0, first:first + x.shape[0]].set(x.astype(F32))


def kernel(x, meta_tokens, ffn1_norm, ffn1_w_gate, ffn1_w_up, ffn1_w_down, mix_norm, w_in, sb_out_norm, dn_conv_w, dn_a_log, dn_dt_bias, dn_out_norm, s5_a_re, s5_a_im, s5_log_dt, s5_b_re, s5_b_im, s5_c_re, s5_c_im, s5_d, s5_w_glu, s5_b_glu, s5_out_norm, w_out, ffn2_norm, ffn2_w_gate, ffn2_w_up, ffn2_w_down, final_norm):
    bsz, seq, d = x.shape
    depth = w_in.shape[0]
    lp = FRONT_PAD + N_META + seq
    ntok = bsz * lp
    assert lp % SB_BLOCK == 0 and lp % DN_CHUNK == 0 and lp % (2 * S5_T) == 0
    meta = jnp.broadcast_to(meta_tokens[None].astype(x.dtype), (bsz, N_META, d))
    h = jnp.concatenate([jnp.zeros((bsz, FRONT_PAD, d), x.dtype), meta, x], axis=1).reshape(ntok, d)
    tok3 = lambda a: a.reshape(bsz, lp, a.shape[-1])
    bf = lambda a: a.astype(BF16)
    for l in range(depth):
        h, sbq, sbk, sbv, dqkv, dz, dba, u5 = _ffn_inproj(
            h, ffn1_norm[l][None], bf(ffn1_w_gate[l]), bf(ffn1_w_up[l]), bf(ffn1_w_down[l]),
            mix_norm[l][None], _inproj_weight(w_in[l]))
        o_sb = _sb_attention(tok3(sbq), tok3(sbk), tok3(sbv), sb_out_norm[l][None])
        o_dn = _deltanet(tok3(dqkv), tok3(dz), tok3(dba), dn_conv_w[l],
                         _lane_vec(dn_a_log[l], N_HEADS), _lane_vec(dn_dt_bias[l], N_HEADS),
                         jnp.tile(dn_out_norm[l], N_HEADS)[None])
        y5 = _s5(tok3(u5), _s5_matrices(s5_a_re[l], s5_a_im[l], s5_log_dt[l], s5_b_re[l], s5_b_im[l],
                                        s5_c_re[l], s5_c_im[l], s5_d[l]))
        tail = (o_sb.reshape(ntok, HW), o_dn.reshape(ntok, HW), y5.reshape(ntok, -1),
                bf(s5_w_glu[l]), s5_b_glu[l][None], s5_out_norm[l][None], bf(w_out[l]),
                ffn2_norm[l][None], bf(ffn2_w_gate[l]), bf(ffn2_w_up[l]), bf(ffn2_w_down[l]))
        if l < depth - 1:
            h = _mixout_ffn(h, *tail)
    out = _mixout_ffn_final(h, *tail, final_norm[None], bsz, FRONT_PAD + N_META)
    return out.reshape(bsz, seq, d)
```

```python
import functools

import jax
import jax.numpy as jnp
from jax import lax
from jax.experimental import pallas as pl
from jax.experimental.pallas import tpu as pltpu

F32 = jnp.float32
BF16 = jnp.bfloat16

N_META = 16
HEAD_DIM = 64
N_HEADS = 4
HW = N_HEADS * HEAD_DIM
SB_BLOCK = 128
SB_WINDOW = 256
DN_CHUNK = 64
DN_CONV = 4
FF_CHUNK = 256
LANES = 128
S5_GROUP = 16
S5_STATE = 64
S5_T = 16
S5_LG = LANES // S5_GROUP
EPS = 1e-6
FRONT_PAD = (-N_META) % SB_BLOCK
EXP_UNDERFLOW = -88.0
VMEM_LIMIT = 56 * 1024 * 1024


def _cparams(sem):
    return pltpu.CompilerParams(dimension_semantics=sem, vmem_limit_bytes=VMEM_LIMIT)


def _pick_tile(n, candidates):
    for c in candidates:
        if n % c == 0:
            return c
    raise ValueError(f"no tile for {n}")


def _rms(x):
    return x * lax.rsqrt(jnp.mean(x * x, axis=-1, keepdims=True) + EPS)


def _split_bf16(x):
    hi = x.astype(BF16)
    lo = (x - hi.astype(F32)).astype(BF16)
    return hi, lo


def _dot(a, b):
    return jnp.dot(a, b, preferred_element_type=F32)


def _dot_nt(a, b):
    return lax.dot_general(a, b, (((1,), (1,)), ((), ())), preferred_element_type=F32)


def _dot_tn(a, b):
    return lax.dot_general(a, b, (((0,), (0,)), ((), ())), preferred_element_type=F32)


def _dot_split(x, m_bf16):
    hi, lo = _split_bf16(x)
    return _dot(hi, m_bf16) + _dot(lo, m_bf16)


def _lockstep(gens):
    results = [None] * len(gens)
    live = list(range(len(gens)))
    while live:
        still = []
        for n in live:
            try:
                next(gens[n])
                still.append(n)
            except StopIteration as stop:
                results[n] = stop.value
        live = still
    return results


def _const_spec(shape):
    nd = len(shape)
    return pl.BlockSpec(shape, lambda *_: (0,) * nd, pipeline_mode=pl.Buffered(1))


def _layer_spec(a, l):
    nd = a.ndim - 1
    return pl.BlockSpec((None,) + a.shape[1:], lambda *_: (l,) + (0,) * nd, pipeline_mode=pl.Buffered(1))


def _swiglu_half_step(x, g_ref, wg_ref, wu_ref, wd_ref, acc_ref):
    xn = (_rms(x) * g_ref[...]).astype(BF16)

    def chunk_out(c):
        cols = pl.ds(pl.multiple_of(c * FF_CHUNK, FF_CHUNK), FF_CHUNK)
        gate = _dot(xn, wg_ref[:, cols])
        up = _dot(xn, wu_ref[:, cols])
        act = (gate * jax.nn.sigmoid(gate) * up).astype(BF16)
        return _dot(act, wd_ref[cols, :])

    acc_ref[...] = chunk_out(0)

    def chunk(c, carry):
        acc_ref[...] += chunk_out(c)
        return carry

    lax.fori_loop(1, wg_ref.shape[1] // FF_CHUNK, chunk, 0)
    return x + 0.5 * acc_ref[...]


_C_SBQ, _C_SBK, _C_SBV, _C_DNQKV, _C_DNZ, _C_DNBA, _C_S5U, _C_END = 0, 256, 512, 768, 1536, 1792, 1920, 2432
_PROJ_WIDTHS = (HW, HW, HW, 3 * HW, HW, LANES, 512)
_PROJ_DTYPES = (BF16, BF16, BF16, F32, F32, F32, F32)


def _ffn_inproj_body(h_ref, g_ref, wg_ref, wu_ref, wd_ref, gm_ref, w_ref,
                     o_ref, q_ref, k_ref, v_ref, dqkv_ref, dz_ref, dba_ref, u_ref, acc_ref):
    h = _swiglu_half_step(h_ref[...], g_ref, wg_ref, wu_ref, wd_ref, acc_ref)
    o_ref[...] = h
    xn = (_rms(h) * gm_ref[...]).astype(BF16)
    q_ref[...] = (_dot(xn, w_ref[:, _C_SBQ:_C_SBK]) * (HEAD_DIM ** -0.5)).astype(BF16)
    k_ref[...] = _dot(xn, w_ref[:, _C_SBK:_C_SBV]).astype(BF16)
    v_ref[...] = _dot(xn, w_ref[:, _C_SBV:_C_DNQKV]).astype(BF16)
    dqkv_ref[...] = _dot(xn, w_ref[:, _C_DNQKV:_C_DNZ])
    dz_ref[...] = _dot(xn, w_ref[:, _C_DNZ:_C_DNBA])
    dba_ref[...] = _dot(xn, w_ref[:, _C_DNBA:_C_S5U])
    u_ref[...] = _dot(xn, w_ref[:, _C_S5U:_C_END])


def _ffn_inproj(h, l, g, wg, wu, wd, gm, w):
    ntok, d = h.shape
    tm = _pick_tile(ntok, (768, 512, 384, 256, 128))
    row = lambda wd_: pl.BlockSpec((tm, wd_), lambda i: (i, 0))
    return pl.pallas_call(
        _ffn_inproj_body,
        grid=(ntok // tm,),
        in_specs=[row(d)] + [_layer_spec(a, l) for a in (g, wg, wu, wd, gm, w)],
        out_specs=[row(d)] + [row(wd_) for wd_ in _PROJ_WIDTHS],
        out_shape=[jax.ShapeDtypeStruct((ntok, d), F32)]
        + [jax.ShapeDtypeStruct((ntok, wd_), dt) for wd_, dt in zip(_PROJ_WIDTHS, _PROJ_DTYPES)],
        scratch_shapes=[pltpu.VMEM((tm, d), F32)],
        compiler_params=_cparams(("parallel",)),
        name="ffn_inproj",
    )(h, g, wg, wu, wd, gm, w)


def _mixout_ffn_body(h_ref, osb_ref, odn_ref, y_ref, wglu_ref, bglu_ref, g5_ref, wout_ref,
                     g_ref, wg_ref, wu_ref, wd_ref, *rest, final):
    if final:
        gf_ref, o_ref, acc_ref = rest
    else:
        o_ref, acc_ref = rest
    y = y_ref[...]
    gate = _dot(y.astype(BF16), wglu_ref[...]) + bglu_ref[...]
    o5 = (_rms(y * jax.nn.sigmoid(gate)) * g5_ref[...]).astype(BF16)
    mixed = (_dot(osb_ref[...], wout_ref[0:HW, :]) + _dot(odn_ref[...], wout_ref[HW:2 * HW, :])
             + _dot(o5, wout_ref[2 * HW:, :]))
    h = _swiglu_half_step(h_ref[...] + mixed, g_ref, wg_ref, wu_ref, wd_ref, acc_ref)
    o_ref[...] = _rms(h) * gf_ref[...] if final else h


def _mixout_ffn(h, osb, odn, y5, l, wglu, bglu, g5, wout, g, wg, wu, wd):
    ntok, d = h.shape
    tm = _pick_tile(ntok, (768, 512, 384, 256, 128))
    row = lambda w: pl.BlockSpec((tm, w), lambda i: (i, 0))
    consts = [wglu, bglu, g5, wout, g, wg, wu, wd]
    return pl.pallas_call(
        functools.partial(_mixout_ffn_body, final=False),
        grid=(ntok // tm,),
        in_specs=[row(d), row(HW), row(HW), row(512)] + [_layer_spec(c, l) for c in consts],
        out_specs=row(d),
        out_shape=jax.ShapeDtypeStruct((ntok, d), F32),
        scratch_shapes=[pltpu.VMEM((tm, d), F32)],
        compiler_params=_cparams(("parallel",)),
        name="mixout_ffn",
    )(h, osb, odn, y5, *consts)


def _mixout_ffn_final(h, osb, odn, y5, l, wglu, bglu, g5, wout, g, wg, wu, wd, gf, bsz, skip):
    ntok, d = h.shape
    lp = ntok // bsz
    seq = lp - skip
    tm = _pick_tile(seq, (1024, 512, 256, 128))
    per = seq // tm
    sub = 8
    assert lp % sub == 0 and skip % sub == 0
    row = lambda w: pl.BlockSpec(
        (pl.Element(tm), pl.Element(w)),
        lambda b, i: ((b * (lp // sub) + skip // sub + i * (tm // sub)) * sub, 0))
    consts = [wglu, bglu, g5, wout, g, wg, wu, wd]
    return pl.pallas_call(
        functools.partial(_mixout_ffn_body, final=True),
        grid=(bsz, per),
        in_specs=[row(d), row(HW), row(HW), row(512)] + [_layer_spec(c, l) for c in consts]
        + [_const_spec(gf.shape)],
        out_specs=pl.BlockSpec((tm, d), lambda b, i: (b * per + i, 0)),
        out_shape=jax.ShapeDtypeStruct((bsz * seq, d), F32),
        scratch_shapes=[pltpu.VMEM((tm, d), F32)],
        compiler_params=_cparams(("parallel", "parallel")),
        name="mixout_ffn_final",
    )(h, osb, odn, y5, *consts, gf)


def _sb_body(q_ref, k_ref, v_ref, sfx_ref, g_ref, o_ref):
    i = pl.program_id(0)
    nb = q_ref.shape[0]
    tq, tk = SB_BLOCK, SB_WINDOW
    row = lax.broadcasted_iota(jnp.int32, (tq, tk), 0)
    col = lax.broadcasted_iota(jnp.int32, (tq, tk), 1)
    qpos = i * tq + row
    chains = [(b, h) for b in range(nb) for h in range(N_HEADS)]
    qs = [q_ref[b, :, h * HEAD_DIM:(h + 1) * HEAD_DIM] for b, h in chains]
    sfx = sfx_ref[...]

    def cond(state):
        w0, done = state[0], state[1]
        return jnp.logical_and(w0 + tk > 0, jnp.logical_not(done))

    def body(state):
        w0, _, carry, acc = state
        start = pl.multiple_of(jnp.maximum(w0, 0), tq)
        kpos = start + col
        valid = jnp.logical_and(kpos < jnp.minimum(qpos, w0 + tk), kpos >= FRONT_PAD)

        def chain(n, b, h):
            hs = slice(h * HEAD_DIM, (h + 1) * HEAD_DIM)
            z = _dot_nt(qs[n], k_ref[b, pl.ds(start, tk), hs])
            yield
            sp = jnp.maximum(z, 0.0) + jnp.log(1.0 + jnp.exp(-jnp.abs(z)))
            lk = jnp.where(valid, -sp, 0.0)
            sums = _dot_split(lk, sfx)
            yield
            c = carry[n]
            expo = z - sp + sums[:, 0:tk] + jnp.concatenate([c] * (tk // LANES), axis=1)
            w = jnp.where(valid, jnp.exp(expo), 0.0)
            pv = _dot(w.astype(BF16), v_ref[b, pl.ds(start, tk), hs])
            yield
            return c + sums[:, tk:], acc[n] + pv

        results = _lockstep([chain(n, b, h) for n, (b, h) in enumerate(chains)])
        new_carry = tuple(r[0] for r in results)
        top = functools.reduce(jnp.maximum, new_carry)
        return w0 - tk, jnp.max(top) < EXP_UNDERFLOW, new_carry, tuple(r[1] for r in results)

    zc = tuple(jnp.zeros((tq, LANES), F32) for _ in chains)
    za = tuple(jnp.zeros((tq, HEAD_DIM), F32) for _ in chains)
    _, _, _, acc = lax.while_loop(cond, body, ((i + 1) * tq - tk, jnp.bool_(False), zc, za))
    gain = g_ref[...]
    for b in range(nb):
        o_ref[b] = jnp.concatenate([_rms(acc[b * N_HEADS + h]) * gain for h in range(N_HEADS)],
                                   axis=1).astype(o_ref.dtype)


def _sb_attention(q, k, v, l, gain):
    bsz, lp, _ = q.shape
    nblk = lp // SB_BLOCK
    tk = SB_WINDOW
    r = jnp.arange(tk)
    sfx = jnp.concatenate([(r[:, None] > r[None, :]), jnp.ones((tk, LANES), bool)], axis=1).astype(BF16)
    full = pl.BlockSpec((bsz, lp, HW), lambda i: (0, 0, 0), pipeline_mode=pl.Buffered(1))
    return pl.pallas_call(
        _sb_body,
        grid=(nblk,),
        in_specs=[pl.BlockSpec((bsz, SB_BLOCK, HW), lambda i: (0, i, 0)), full, full,
                  _const_spec(sfx.shape), _layer_spec(gain, l)],
        out_specs=pl.BlockSpec((bsz, SB_BLOCK, HW), lambda i: (0, i, 0)),
        out_shape=jax.ShapeDtypeStruct((bsz, lp, HW), BF16),
        compiler_params=_cparams(("parallel",)),
        name="sb_attention",
    )(q, k, v, sfx, gain)


def _head_mask(rows, cols, rdiv, cdiv):
    r = lax.broadcasted_iota(jnp.int32, (rows, cols), 0) // rdiv
    c = lax.broadcasted_iota(jnp.int32, (rows, cols), 1) // cdiv
    return r == c


def _dn_body(qkv_ref, halo_ref, z_ref, ba_ref, cw_ref, alog_ref, dtb_ref, gain_ref, o_ref,
             q_s, k_s, kb_s, vb_s, gs_s, gc_s, u_s, wq_s, a_s, kg_s, o_s, state_s, *, tb):
    i = pl.program_id(0)
    nb = qkv_ref.shape[0]
    c = DN_CHUNK

    @pl.when(i == 0)
    def _():
        state_s[...] = jnp.zeros_like(state_s)

    bd_mask = _head_mask(HW, HW, HEAD_DIM, HEAD_DIM)
    ones_bd = jnp.where(bd_mask, 1.0, 0.0).astype(BF16)
    rowi = lax.broadcasted_iota(jnp.int32, (c, HW), 0)
    colj = lax.broadcasted_iota(jnp.int32, (c, HW), 1) % HEAD_DIM
    strict = rowi > colj
    incl = rowi >= colj
    eye = jnp.where(rowi == colj, 1.0, 0.0)
    strict3 = jnp.concatenate([strict] * 3, axis=1)
    tri = jnp.where(lax.broadcasted_iota(jnp.int32, (c, c), 0) >= lax.broadcasted_iota(jnp.int32, (c, c), 1),
                    1.0, 0.0).astype(BF16)
    rt = lax.broadcasted_iota(jnp.int32, (tb, tb), 0)
    ct = lax.broadcasted_iota(jnp.int32, (tb, tb), 1)
    tri_blk = jnp.where(jnp.logical_and(rt // c == ct // c, rt >= ct), 1.0, 0.0).astype(BF16)

    def headsum(x):
        return _dot_split(x, ones_bd)

    def blockdiag(y):
        return jnp.where(bd_mask, jnp.concatenate([y] * N_HEADS, axis=0), jnp.zeros((), y.dtype))

    def lanes_of(x, first):
        return jnp.concatenate(
            [jnp.broadcast_to(x[:, first + h:first + h + 1], (tb, HEAD_DIM)) for h in range(N_HEADS)], axis=1)

    def sum3(x):
        return x[:, 0:HW] + x[:, HW:2 * HW] + x[:, 2 * HW:3 * HW]

    cw = cw_ref[...]
    real = (i * tb + lax.broadcasted_iota(jnp.int32, (tb, 1), 0)) >= FRONT_PAD
    for b in range(nb):
        x = qkv_ref[b]
        halo = jnp.where(i > 0, halo_ref[b], 0.0)
        xx = jnp.concatenate([halo, x], axis=0)
        conv = (cw[3:4] * x + cw[2:3] * xx[7:7 + tb] + cw[1:2] * xx[6:6 + tb] + cw[0:1] * xx[5:5 + tb])
        y = conv * jax.nn.sigmoid(conv)
        qr, kr, v = y[:, 0:HW], y[:, HW:2 * HW], y[:, 2 * HW:3 * HW]
        q = qr * lax.rsqrt(headsum(qr * qr) + EPS) * (HEAD_DIM ** -0.5)
        k = kr * lax.rsqrt(headsum(kr * kr) + EPS)
        ba = ba_ref[b]
        beta = jnp.where(real, jax.nn.sigmoid(ba), 0.0)
        gate = jnp.where(real, -jnp.exp(alog_ref[...]) * jax.nn.softplus(ba + dtb_ref[...]), 0.0)
        betab = lanes_of(beta, 0)
        gb = lanes_of(gate, N_HEADS)
        g1 = gb.astype(BF16)
        r1 = gb - g1.astype(F32)
        g2 = r1.astype(BF16)
        g3 = (r1 - g2.astype(F32)).astype(BF16)
        gs = jnp.concatenate([g1, g2, g3], axis=1)
        q_s[b] = q
        k_s[b] = k
        kb_s[b] = k * betab
        vb_s[b] = v * betab
        gs_s[b] = gs
        gc_s[b] = sum3(_dot(tri_blk, gs))

    def chunk_matrices(q, k, kb, vb, gc, gs):
        dm = sum3(_dot(tri, jnp.where(strict3, gs, jnp.zeros((), BF16))))
        decay = jnp.exp(dm)
        kq = _dot_nt(jnp.concatenate([kb, q], axis=0).astype(BF16), blockdiag(k.astype(BF16)))
        yield
        lmat = jnp.where(strict, kq[0:c] * decay, 0.0)
        attn = jnp.where(incl, kq[c:] * decay, 0.0)
        neg = -lmat
        ph, plo = _split_bf16(neg)
        res = _dot(jnp.concatenate([ph, plo], axis=0), blockdiag(ph))
        res2 = _dot(ph, blockdiag(plo))
        yield
        power = res[0:c] + res[c:] + res2
        tinv = eye + neg
        for step in range(1, 6):
            ph, plo = _split_bf16(power)
            th, tlo = _split_bf16(tinv)
            bh, bl = blockdiag(ph), blockdiag(plo)
            if step < 5:
                res = _dot(jnp.concatenate([ph, plo, th, tlo], axis=0), bh)
                res2 = _dot(jnp.concatenate([ph, th], axis=0), bl)
                yield
                power = res[0:c] + res[c:2 * c] + res2[0:c]
                tinv = tinv + (res[2 * c:3 * c] + res[3 * c:] + res2[c:])
            else:
                res = _dot(jnp.concatenate([th, tlo], axis=0), bh)
                res2 = _dot(th, bl)
                yield
                tinv = tinv + (res[0:c] + res[c:] + res2)
        egc = jnp.exp(gc)
        uw = _dot(tinv.astype(BF16),
                  jnp.concatenate([blockdiag(vb.astype(BF16)), blockdiag((kb * egc).astype(BF16))], axis=1))
        return (uw[:, 0:HW], jnp.concatenate([uw[:, HW:], q * egc], axis=0).astype(BF16), attn.astype(BF16),
                (k * jnp.exp(gc[c - 1:c, :] - gc)).astype(BF16))

    per_trip = 2 if (tb // c) % 2 == 0 else 1

    def matrices(ti, carry):
        chains = [(b, ti * per_trip + n) for n in range(per_trip) for b in range(nb)]
        rows = [pl.ds(pl.multiple_of(ci * c, c), c) for _, ci in chains]
        loaded = [tuple(s[b, r, :] for s in (q_s, k_s, kb_s, vb_s, gc_s, gs_s)) for (b, _), r in zip(chains, rows)]
        results = _lockstep([chunk_matrices(*vals) for vals in loaded])
        for (b, ci), r, (u, wq, attn, kg) in zip(chains, rows, results):
            u_s[b, r, :] = u
            wq_s[b, pl.ds(pl.multiple_of(ci * 2 * c, 2 * c), 2 * c), :] = wq
            a_s[b, r, :] = attn
            kg_s[b, r, :] = kg
        return carry

    lax.fori_loop(0, tb // (c * per_trip), matrices, 0)

    def recur(ci, carry):
        r = pl.ds(pl.multiple_of(ci * c, c), c)
        r2 = pl.ds(pl.multiple_of(ci * 2 * c, 2 * c), 2 * c)
        loaded = [(state_s[b], wq_s[b, r2, :], u_s[b, r, :], a_s[b, r, :], kg_s[b, r, :],
                   gc_s[b, pl.ds(ci * c + c - 1, 1), :]) for b in range(nb)]

        def chunk_step(state, lhs, u, attn, kg, g_last):
            wq = _dot(lhs, state.astype(BF16))
            yield
            v_new = (u - wq[0:c]).astype(BF16)
            av = _dot(attn, blockdiag(v_new))
            upd = _dot_tn(kg, v_new)
            yield
            return wq[c:] + av, state * jnp.exp(g_last) + jnp.where(bd_mask, upd, 0.0)

        results = _lockstep([chunk_step(*vals) for vals in loaded])
        for b, (o, state) in enumerate(results):
            o_s[b, r, :] = o
            state_s[b] = state
        return carry

    lax.fori_loop(0, tb // c, recur, 0)

    gain = gain_ref[...]
    for b in range(nb):
        o = o_s[b]
        zz = z_ref[b]
        on = o * lax.rsqrt(headsum(o * o) * (1.0 / HEAD_DIM) + EPS) * gain
        o_ref[b] = (on * (zz * jax.nn.sigmoid(zz))).astype(o_ref.dtype)


def _deltanet(qkv, z, ba, l, conv_w, alog, dtb, gain):
    bsz, lp, _ = qkv.shape
    tb = _pick_tile(lp, (384, 128))
    halo_blocks = tb // 8
    row = lambda w: pl.BlockSpec((bsz, tb, w), lambda i: (0, i, 0))
    f32s = pltpu.VMEM((bsz, tb, HW), F32)
    b16s = pltpu.VMEM((bsz, tb, HW), BF16)
    return pl.pallas_call(
        functools.partial(_dn_body, tb=tb),
        grid=(lp // tb,),
        in_specs=[row(3 * HW),
                  pl.BlockSpec((bsz, 8, 3 * HW), lambda i: (0, jnp.maximum(i * halo_blocks - 1, 0), 0)),
                  row(HW), row(LANES),
                  _layer_spec(conv_w, l), _layer_spec(alog, l), _layer_spec(dtb, l), _layer_spec(gain, l)],
        out_specs=row(HW),
        out_shape=jax.ShapeDtypeStruct((bsz, lp, HW), BF16),
        scratch_shapes=[f32s, f32s, f32s, f32s,
                        pltpu.VMEM((bsz, tb, 3 * HW), BF16), f32s,
                        f32s, pltpu.VMEM((bsz, 2 * tb, HW), BF16), b16s, b16s,
                        f32s, pltpu.VMEM((bsz, HW, HW), F32)],
        compiler_params=_cparams(("arbitrary",)),
        name="deltanet",
    )(qkv, qkv, z, ba, conv_w, alog, dtb, gain)


def _s5_body(u_ref, m_ref, w_ref, v_ref, ar_ref, ai_ref, d_ref, y_ref, e_s, sp_s, st_s):
    n = u_ref.shape[0] // S5_T
    half = S5_LG * S5_STATE

    @pl.when(pl.program_id(2) == 0)
    def _():
        st_s[...] = jnp.zeros_like(st_s)

    xs = [u_ref[pl.ds(t, n, stride=S5_T), :] for t in range(S5_T)]
    xcat = jnp.concatenate(xs, axis=1).astype(BF16)
    e_s[...] = _dot(xcat, w_ref[...])
    ar = ar_ref[...]
    ai = ai_ref[...]

    def step(ci, s):
        sr, si = s
        row = pl.ds(ci, 1)
        sp_s[row, 0:half] = sr
        sp_s[row, half:] = si
        er = e_s[row, 0:half]
        ei = e_s[row, half:]
        return ar * sr - ai * si + er, ar * si + ai * sr + ei

    sr, si = lax.fori_loop(0, n, step, (st_s[0:1, :], st_s[1:2, :]), unroll=4)
    st_s[0:1, :] = sr
    st_s[1:2, :] = si
    sp = sp_s[...].astype(BF16)
    d = d_ref[...]
    cb = 2 * LANES
    for k in range(S5_T * LANES // cb):
        cols = slice(k * cb, (k + 1) * cb)
        yk = _dot(xcat[:, 0:(k + 1) * cb], m_ref[0:(k + 1) * cb, cols]) + _dot(sp, v_ref[:, cols])
        for t in range(2 * k, 2 * k + 2):
            off = (t - 2 * k) * LANES
            y_ref[pl.ds(t, n, stride=S5_T), :] = jax.nn.gelu(yk[:, off:off + LANES] + d * xs[t])


def _s5(u, l, mats):
    m, w, v, ar, ai, d = mats
    bsz, lp, width = u.shape
    ntile = width // LANES
    rows = _pick_tile(lp, (lp // 2,))
    n = rows // S5_T
    mat = lambda a: pl.BlockSpec((None, None) + a.shape[2:], lambda j, b, r: (l, j) + (0,) * (a.ndim - 2),
                                 pipeline_mode=pl.Buffered(1))
    return pl.pallas_call(
        _s5_body,
        grid=(ntile, bsz, lp // rows),
        in_specs=[pl.BlockSpec((None, rows, LANES), lambda j, b, r: (b, r, j)),
                  mat(m), mat(w), mat(v), mat(ar), mat(ai), mat(d)],
        out_specs=pl.BlockSpec((None, rows, LANES), lambda j, b, r: (b, r, j)),
        out_shape=jax.ShapeDtypeStruct(u.shape, F32),
        scratch_shapes=[pltpu.VMEM((n, 2 * S5_LG * S5_STATE), F32), pltpu.VMEM((n, 2 * S5_LG * S5_STATE), F32),
                        pltpu.VMEM((8, S5_LG * S5_STATE), F32)],
        compiler_params=_cparams(("arbitrary", "arbitrary", "arbitrary")),
        name="s5",
    )(u, m, w, v, ar, ai, d)


def _s5_matrices(a_re, a_im, log_dt, b_re, b_im, c_re, c_im, d):
    ng, p = a_re.shape
    t, lg, cg = S5_T, S5_LG, S5_GROUP
    nj = ng // lg
    dt = jnp.exp(log_dt)[:, None]
    lr, li = a_re * dt, a_im * dt
    er = jnp.exp(lr)
    abr, abi = er * jnp.cos(li), er * jnp.sin(li)
    den = a_re * a_re + a_im * a_im
    fr = ((abr - 1.0) * a_re + abi * a_im) / den
    fi = (abi * a_re - (abr - 1.0) * a_im) / den
    bbr = fr[..., None] * b_re - fi[..., None] * b_im
    bbi = fr[..., None] * b_im + fi[..., None] * b_re
    n = jnp.arange(t + 1, dtype=F32)[:, None, None]
    pr = jnp.exp(n * lr[None]) * jnp.cos(n * li[None])
    pi = jnp.exp(n * lr[None]) * jnp.sin(n * li[None])
    car = c_re[None] * pr[:, :, None, :] - c_im[None] * pi[:, :, None, :]
    cai = c_re[None] * pi[:, :, None, :] + c_im[None] * pr[:, :, None, :]
    kern = jnp.einsum('ngop,gpi->ngio', car[:t], bbr) - jnp.einsum('ngop,gpi->ngio', cai[:t], bbi)
    lag = jnp.arange(t)[None, :, None] - jnp.arange(t)[:, None, None]
    select = (lag == jnp.arange(t)[None, None, :]).astype(F32)
    mg = jnp.einsum('aun,ngio->gaiuo', select, kern, precision=lax.Precision.HIGHEST)
    mg = mg.reshape(nj, lg, t * cg, t * cg).astype(BF16)
    nrev = (t - 1) - jnp.arange(t, dtype=F32)[:, None, None]
    rev_r = (jnp.exp(nrev * lr[None]) * jnp.cos(nrev * li[None]))[..., None]
    rev_i = (jnp.exp(nrev * lr[None]) * jnp.sin(nrev * li[None]))[..., None]
    wg = jnp.stack([rev_r * bbr[None] - rev_i * bbi[None], rev_r * bbi[None] + rev_i * bbr[None]])
    wg = jnp.transpose(wg, (2, 1, 4, 0, 3)).reshape(nj, lg, t * cg, 2 * p).astype(BF16)
    vg = jnp.stack([car[1:], -cai[1:]])
    vg = jnp.transpose(vg, (2, 0, 4, 1, 3)).reshape(nj, lg, 2 * p, t * cg).astype(BF16)

    def placement(rows, cols, col_key, col_group):
        rr = lax.broadcasted_iota(jnp.int32, (lg, rows, cols), 1)
        cc = lax.broadcasted_iota(jnp.int32, (lg, rows, cols), 2)
        gg = lax.broadcasted_iota(jnp.int32, (lg, rows, cols), 0)
        return jnp.logical_and(rr == col_key(cc), col_group(cc) == gg).astype(BF16)

    p_out = placement(t * cg, t * LANES, lambda c: (c // LANES) * cg + c % cg, lambda c: (c // cg) % lg)
    p_state = placement(2 * p, 2 * lg * p, lambda c: (c // (lg * p)) * p + c % p, lambda c: (c // p) % lg)
    place = lambda x, pm: jnp.einsum('jrkc,rcn->jrkn', x, pm, preferred_element_type=BF16)
    m = jnp.transpose(place(mg, p_out).reshape(nj, lg, t, cg, t * LANES), (0, 2, 1, 3, 4)).reshape(
        nj, t * LANES, t * LANES)
    w = jnp.transpose(place(wg, p_state).reshape(nj, lg, t, cg, 2 * lg * p), (0, 2, 1, 3, 4)).reshape(
        nj, t * LANES, 2 * lg * p)
    v = jnp.transpose(place(vg, p_out).reshape(nj, lg, 2, p, t * LANES), (0, 2, 1, 3, 4)).reshape(
        nj, 2 * lg * p, t * LANES)
    ar = pr[t].reshape(nj, 1, lg * p)
    ai = pi[t].reshape(nj, 1, lg * p)
    return m, w, v, ar, ai, d.reshape(nj, 1, LANES)


def _inproj_weight(w_in):
    pad = jnp.zeros(w_in.shape[:-1] + (LANES - 2 * N_HEADS,), w_in.dtype)
    return jnp.concatenate([w_in[..., 0:1800], pad, w_in[..., 1800:2312]], axis=-1).astype(BF16)


def kernel(x, meta_tokens, ffn1_norm, ffn1_w_gate, ffn1_w_up, ffn1_w_down, mix_norm, w_in, sb_out_norm, dn_conv_w, dn_a_log, dn_dt_bias, dn_out_norm, s5_a_re, s5_a_im, s5_log_dt, s5_b_re, s5_b_im, s5_c_re, s5_c_im, s5_d, s5_w_glu, s5_b_glu, s5_out_norm, w_out, ffn2_norm, ffn2_w_gate, ffn2_w_up, ffn2_w_down, final_norm):
    bsz, seq, d = x.shape
    depth = w_in.shape[0]
    lp = FRONT_PAD + N_META + seq
    ntok = bsz * lp
    assert lp % SB_BLOCK == 0 and lp % DN_CHUNK == 0 and lp % (2 * S5_T) == 0
    meta = jnp.broadcast_to(meta_tokens[None].astype(x.dtype), (bsz, N_META, d))
    h = jnp.concatenate([jnp.zeros((bsz, FRONT_PAD, d), x.dtype), meta, x], axis=1).reshape(ntok, d)
    tok3 = lambda a: a.reshape(bsz, lp, a.shape[-1])
    bf = lambda a: a.astype(BF16)
    vec = lambda a: a.astype(F32)[:, None, :]
    lane = lambda a: jnp.zeros((depth, 1, LANES), F32).at[:, 0, N_HEADS:2 * N_HEADS].set(a.astype(F32))
    ffn1 = (vec(ffn1_norm), bf(ffn1_w_gate), bf(ffn1_w_up), bf(ffn1_w_down), vec(mix_norm), _inproj_weight(w_in))
    sb_gain = vec(sb_out_norm)
    dn = (dn_conv_w.astype(F32), lane(dn_a_log), lane(dn_dt_bias), vec(jnp.tile(dn_out_norm, (1, N_HEADS))))
    s5_mats = jax.vmap(_s5_matrices)(s5_a_re, s5_a_im, s5_log_dt, s5_b_re, s5_b_im, s5_c_re, s5_c_im, s5_d)
    tail = (bf(s5_w_glu), vec(s5_b_glu), vec(s5_out_norm), bf(w_out),
            vec(ffn2_norm), bf(ffn2_w_gate), bf(ffn2_w_up), bf(ffn2_w_down))
    for l in range(depth):
        h, sbq, sbk, sbv, dqkv, dz, dba, u5 = _ffn_inproj(h, l, *ffn1)
        o_sb = _sb_attention(tok3(sbq), tok3(sbk), tok3(sbv), l, sb_gain)
        o_dn = _deltanet(tok3(dqkv), tok3(dz), tok3(dba), l, *dn)
        y5 = _s5(tok3(u5), l, s5_mats)
        branches = (o_sb.reshape(ntok, HW), o_dn.reshape(ntok, HW), y5.reshape(ntok, -1))
        if l < depth - 1:
            h = _mixout_ffn(h, *branches, l, *tail)
    out = _mixout_ffn_final(h, *branches, depth - 1, *tail, final_norm[None], bsz, FRONT_PAD + N_META)
    return out.reshape(bsz, seq, d)
```

```python
import functools

import jax
import jax.numpy as jnp
from jax import lax
from jax.experimental import pallas as pl
from jax.experimental.pallas import tpu as pltpu

F32 = jnp.float32
BF16 = jnp.bfloat16

N_META = 16
HEAD_DIM = 64
N_HEADS = 4
HW = N_HEADS * HEAD_DIM
SB_BLOCK = 128
SB_WINDOW = 256
DN_CHUNK = 64
DN_CONV = 4
FF_CHUNK = 256
LANES = 128
S5_GROUP = 16
S5_STATE = 64
S5_T = 16
S5_LG = LANES // S5_GROUP
EPS = 1e-6
FRONT_PAD = (-N_META) % SB_BLOCK
EXP_UNDERFLOW = -88.0
VMEM_LIMIT = 56 * 1024 * 1024


def _cparams(sem):
    return pltpu.CompilerParams(dimension_semantics=sem, vmem_limit_bytes=VMEM_LIMIT)


def _pick_tile(n, candidates):
    for c in candidates:
        if n % c == 0:
            return c
    raise ValueError(f"no tile for {n}")


def _rms(x):
    return x * lax.rsqrt(jnp.mean(x * x, axis=-1, keepdims=True) + EPS)


def _split_bf16(x):
    hi = x.astype(BF16)
    lo = (x - hi.astype(F32)).astype(BF16)
    return hi, lo


def _dot(a, b):
    return jnp.dot(a, b, preferred_element_type=F32)


def _dot_nt(a, b):
    return lax.dot_general(a, b, (((1,), (1,)), ((), ())), preferred_element_type=F32)


def _dot_tn(a, b):
    return lax.dot_general(a, b, (((0,), (0,)), ((), ())), preferred_element_type=F32)


def _dot_split(x, m_bf16):
    hi, lo = _split_bf16(x)
    return _dot(hi, m_bf16) + _dot(lo, m_bf16)


def _lockstep(gens):
    results = [None] * len(gens)
    live = list(range(len(gens)))
    while live:
        still = []
        for n in live:
            try:
                next(gens[n])
                still.append(n)
            except StopIteration as stop:
                results[n] = stop.value
        live = still
    return results


def _const_spec(shape):
    nd = len(shape)
    return pl.BlockSpec(shape, lambda *_: (0,) * nd, pipeline_mode=pl.Buffered(1))


def _layer_spec(a, l):
    nd = a.ndim - 1
    return pl.BlockSpec((None,) + a.shape[1:], lambda *_: (l,) + (0,) * nd, pipeline_mode=pl.Buffered(1))


def _swiglu_half_step(x, g_ref, wg_ref, wu_ref, wd_ref, acc_ref):
    xn = (_rms(x) * g_ref[...]).astype(BF16)

    def chunk_out(c):
        cols = pl.ds(pl.multiple_of(c * FF_CHUNK, FF_CHUNK), FF_CHUNK)
        gate = _dot(xn, wg_ref[:, cols])
        up = _dot(xn, wu_ref[:, cols])
        act = (gate * jax.nn.sigmoid(gate) * up).astype(BF16)
        return _dot(act, wd_ref[cols, :])

    acc_ref[...] = chunk_out(0)

    def chunk(c, carry):
        acc_ref[...] += chunk_out(c)
        return carry

    lax.fori_loop(1, wg_ref.shape[1] // FF_CHUNK, chunk, 0, unroll=5)
    return x + 0.5 * acc_ref[...]


_C_SBQ, _C_SBK, _C_SBV, _C_DNQKV, _C_DNZ, _C_DNBA, _C_S5U, _C_END = 0, 256, 512, 768, 1536, 1792, 1920, 2432
_PROJ_WIDTHS = (HW, HW, HW, 3 * HW, HW, LANES, 512)
_PROJ_DTYPES = (BF16, BF16, BF16, F32, F32, F32, F32)


def _ffn_inproj_body(h_ref, g_ref, wg_ref, wu_ref, wd_ref, gm_ref, w_ref,
                     o_ref, q_ref, k_ref, v_ref, dqkv_ref, dz_ref, dba_ref, u_ref, acc_ref):
    h = _swiglu_half_step(h_ref[...], g_ref, wg_ref, wu_ref, wd_ref, acc_ref)
    o_ref[...] = h
    xn = (_rms(h) * gm_ref[...]).astype(BF16)
    q_ref[...] = (_dot(xn, w_ref[:, _C_SBQ:_C_SBK]) * (HEAD_DIM ** -0.5)).astype(BF16)
    k_ref[...] = _dot(xn, w_ref[:, _C_SBK:_C_SBV]).astype(BF16)
    v_ref[...] = _dot(xn, w_ref[:, _C_SBV:_C_DNQKV]).astype(BF16)
    dqkv_ref[...] = _dot(xn, w_ref[:, _C_DNQKV:_C_DNZ])
    dz_ref[...] = _dot(xn, w_ref[:, _C_DNZ:_C_DNBA])
    dba_ref[...] = _dot(xn, w_ref[:, _C_DNBA:_C_S5U])
    u_ref[...] = _dot(xn, w_ref[:, _C_S5U:_C_END])


def _ffn_inproj(h, l, g, wg, wu, wd, gm, w):
    ntok, d = h.shape
    tm = _pick_tile(ntok, (768, 512, 384, 256, 128))
    row = lambda wd_: pl.BlockSpec((tm, wd_), lambda i: (i, 0))
    return pl.pallas_call(
        _ffn_inproj_body,
        grid=(ntok // tm,),
        in_specs=[row(d)] + [_layer_spec(a, l) for a in (g, wg, wu, wd, gm, w)],
        out_specs=[row(d)] + [row(wd_) for wd_ in _PROJ_WIDTHS],
        out_shape=[jax.ShapeDtypeStruct((ntok, d), F32)]
        + [jax.ShapeDtypeStruct((ntok, wd_), dt) for wd_, dt in zip(_PROJ_WIDTHS, _PROJ_DTYPES)],
        scratch_shapes=[pltpu.VMEM((tm, d), F32)],
        compiler_params=_cparams(("parallel",)),
        name="ffn_inproj",
    )(h, g, wg, wu, wd, gm, w)


def _mixout_ffn_body(h_ref, osb_ref, odn_ref, y_ref, wglu_ref, bglu_ref, g5_ref, wout_ref,
                     g_ref, wg_ref, wu_ref, wd_ref, *rest, final):
    if final:
        gf_ref, o_ref, acc_ref = rest
    else:
        o_ref, acc_ref = rest
    y = y_ref[...]
    gate = _dot(y.astype(BF16), wglu_ref[...]) + bglu_ref[...]
    o5 = (_rms(y * jax.nn.sigmoid(gate)) * g5_ref[...]).astype(BF16)
    mixed = (_dot(osb_ref[...], wout_ref[0:HW, :]) + _dot(odn_ref[...], wout_ref[HW:2 * HW, :])
             + _dot(o5, wout_ref[2 * HW:, :]))
    h = _swiglu_half_step(h_ref[...] + mixed, g_ref, wg_ref, wu_ref, wd_ref, acc_ref)
    o_ref[...] = _rms(h) * gf_ref[...] if final else h


def _mixout_ffn(h, osb, odn, y5, l, wglu, bglu, g5, wout, g, wg, wu, wd):
    ntok, d = h.shape
    tm = _pick_tile(ntok, (768, 512, 384, 256, 128))
    row = lambda w: pl.BlockSpec((tm, w), lambda i: (i, 0))
    consts = [wglu, bglu, g5, wout, g, wg, wu, wd]
    return pl.pallas_call(
        functools.partial(_mixout_ffn_body, final=False),
        grid=(ntok // tm,),
        in_specs=[row(d), row(HW), row(HW), row(512)] + [_layer_spec(c, l) for c in consts],
        out_specs=row(d),
        out_shape=jax.ShapeDtypeStruct((ntok, d), F32),
        scratch_shapes=[pltpu.VMEM((tm, d), F32)],
        compiler_params=_cparams(("parallel",)),
        name="mixout_ffn",
    )(h, osb, odn, y5, *consts)


def _mixout_ffn_final(h, osb, odn, y5, l, wglu, bglu, g5, wout, g, wg, wu, wd, gf, bsz, skip):
    ntok, d = h.shape
    lp = ntok // bsz
    seq = lp - skip
    tm = _pick_tile(seq, (1024, 512, 256, 128))
    per = seq // tm
    sub = 8
    assert lp % sub == 0 and skip % sub == 0
    row = lambda w: pl.BlockSpec(
        (pl.Element(tm), pl.Element(w)),
        lambda b, i: ((b * (lp // sub) + skip // sub + i * (tm // sub)) * sub, 0))
    consts = [wglu, bglu, g5, wout, g, wg, wu, wd]
    return pl.pallas_call(
        functools.partial(_mixout_ffn_body, final=True),
        grid=(bsz, per),
        in_specs=[row(d), row(HW), row(HW), row(512)] + [_layer_spec(c, l) for c in consts]
        + [_const_spec(gf.shape)],
        out_specs=pl.BlockSpec((tm, d), lambda b, i: (b * per + i, 0)),
        out_shape=jax.ShapeDtypeStruct((bsz * seq, d), F32),
        scratch_shapes=[pltpu.VMEM((tm, d), F32)],
        compiler_params=_cparams(("parallel", "parallel")),
        name="mixout_ffn_final",
    )(h, osb, odn, y5, *consts, gf)


def _sb_body(q_ref, k_ref, v_ref, sfx_ref, g_ref, o_ref):
    i = pl.program_id(0)
    nb = q_ref.shape[0]
    tq, tk = SB_BLOCK, SB_WINDOW
    row = lax.broadcasted_iota(jnp.int32, (tq, tk), 0)
    col = lax.broadcasted_iota(jnp.int32, (tq, tk), 1)
    qpos = i * tq + row
    chains = [(b, h) for b in range(nb) for h in range(N_HEADS)]
    qs = [q_ref[b, :, h * HEAD_DIM:(h + 1) * HEAD_DIM] for b, h in chains]
    sfx = sfx_ref[...]

    def cond(state):
        w0, done = state[0], state[1]
        return jnp.logical_and(w0 + tk > 0, jnp.logical_not(done))

    def body(state):
        w0, _, carry, acc = state
        start = pl.multiple_of(jnp.maximum(w0, 0), tq)
        kpos = start + col
        valid = jnp.logical_and(kpos < jnp.minimum(qpos, w0 + tk), kpos >= FRONT_PAD)

        def chain(n, b, h):
            hs = slice(h * HEAD_DIM, (h + 1) * HEAD_DIM)
            z = _dot_nt(qs[n], k_ref[b, pl.ds(start, tk), hs])
            yield
            sp = jnp.maximum(z, 0.0) + jnp.log(1.0 + jnp.exp(-jnp.abs(z)))
            lk = jnp.where(valid, -sp, 0.0)
            sums = _dot_split(lk, sfx)
            yield
            c = carry[n]
            expo = z - sp + sums[:, 0:tk] + jnp.concatenate([c] * (tk // LANES), axis=1)
            w = jnp.where(valid, jnp.exp(expo), 0.0)
            pv = _dot(w.astype(BF16), v_ref[b, pl.ds(start, tk), hs])
            yield
            return c + sums[:, tk:], acc[n] + pv

        results = _lockstep([chain(n, b, h) for n, (b, h) in enumerate(chains)])
        new_carry = tuple(r[0] for r in results)
        top = functools.reduce(jnp.maximum, new_carry)
        return w0 - tk, jnp.max(top) < EXP_UNDERFLOW, new_carry, tuple(r[1] for r in results)

    zc = tuple(jnp.zeros((tq, LANES), F32) for _ in chains)
    za = tuple(jnp.zeros((tq, HEAD_DIM), F32) for _ in chains)
    _, _, _, acc = lax.while_loop(cond, body, ((i + 1) * tq - tk, jnp.bool_(False), zc, za))
    gain = g_ref[...]
    for b in range(nb):
        o_ref[b] = jnp.concatenate([_rms(acc[b * N_HEADS + h]) * gain for h in range(N_HEADS)],
                                   axis=1).astype(o_ref.dtype)


def _sb_attention(q, k, v, l, gain):
    bsz, lp, _ = q.shape
    nblk = lp // SB_BLOCK
    tk = SB_WINDOW
    r = jnp.arange(tk)
    sfx = jnp.concatenate([(r[:, None] > r[None, :]), jnp.ones((tk, LANES), bool)], axis=1).astype(BF16)
    full = pl.BlockSpec((bsz, lp, HW), lambda i: (0, 0, 0), pipeline_mode=pl.Buffered(1))
    return pl.pallas_call(
        _sb_body,
        grid=(nblk,),
        in_specs=[pl.BlockSpec((bsz, SB_BLOCK, HW), lambda i: (0, i, 0)), full, full,
                  _const_spec(sfx.shape), _layer_spec(gain, l)],
        out_specs=pl.BlockSpec((bsz, SB_BLOCK, HW), lambda i: (0, i, 0)),
        out_shape=jax.ShapeDtypeStruct((bsz, lp, HW), BF16),
        compiler_params=_cparams(("parallel",)),
        name="sb_attention",
    )(q, k, v, sfx, gain)


def _head_mask(rows, cols, rdiv, cdiv):
    r = lax.broadcasted_iota(jnp.int32, (rows, cols), 0) // rdiv
    c = lax.broadcasted_iota(jnp.int32, (rows, cols), 1) // cdiv
    return r == c


def _dn_body(qkv_ref, halo_ref, z_ref, ba_ref, cw_ref, alog_ref, dtb_ref, gain_ref, o_ref,
             q_s, k_s, kb_s, vb_s, gs_s, gc_s, u_s, wq_s, a_s, kg_s, o_s, state_s, *, tb):
    i = pl.program_id(0)
    nb = qkv_ref.shape[0]
    c = DN_CHUNK

    @pl.when(i == 0)
    def _():
        state_s[...] = jnp.zeros_like(state_s)

    bd_mask = _head_mask(HW, HW, HEAD_DIM, HEAD_DIM)
    ones_bd = jnp.where(bd_mask, 1.0, 0.0).astype(BF16)
    rowi = lax.broadcasted_iota(jnp.int32, (c, HW), 0)
    colj = lax.broadcasted_iota(jnp.int32, (c, HW), 1) % HEAD_DIM
    strict = rowi > colj
    incl = rowi >= colj
    eye = jnp.where(rowi == colj, 1.0, 0.0)
    strict3 = jnp.concatenate([strict] * 3, axis=1)
    tri = jnp.where(lax.broadcasted_iota(jnp.int32, (c, c), 0) >= lax.broadcasted_iota(jnp.int32, (c, c), 1),
                    1.0, 0.0).astype(BF16)
    rt = lax.broadcasted_iota(jnp.int32, (tb, tb), 0)
    ct = lax.broadcasted_iota(jnp.int32, (tb, tb), 1)
    tri_blk = jnp.where(jnp.logical_and(rt // c == ct // c, rt >= ct), 1.0, 0.0).astype(BF16)

    def headsum(x):
        return _dot_split(x, ones_bd)

    def blockdiag(y):
        return jnp.where(bd_mask, jnp.concatenate([y] * N_HEADS, axis=0), jnp.zeros((), y.dtype))

    def lanes_of(x, first):
        return jnp.concatenate(
            [jnp.broadcast_to(x[:, first + h:first + h + 1], (tb, HEAD_DIM)) for h in range(N_HEADS)], axis=1)

    def sum3(x):
        return x[:, 0:HW] + x[:, HW:2 * HW] + x[:, 2 * HW:3 * HW]

    cw = cw_ref[...]
    real = (i * tb + lax.broadcasted_iota(jnp.int32, (tb, 1), 0)) >= FRONT_PAD
    for b in range(nb):
        x = qkv_ref[b]
        halo = jnp.where(i > 0, halo_ref[b], 0.0)
        xx = jnp.concatenate([halo, x], axis=0)
        conv = (cw[3:4] * x + cw[2:3] * xx[7:7 + tb] + cw[1:2] * xx[6:6 + tb] + cw[0:1] * xx[5:5 + tb])
        y = conv * jax.nn.sigmoid(conv)
        qr, kr, v = y[:, 0:HW], y[:, HW:2 * HW], y[:, 2 * HW:3 * HW]
        q = qr * lax.rsqrt(headsum(qr * qr) + EPS) * (HEAD_DIM ** -0.5)
        k = kr * lax.rsqrt(headsum(kr * kr) + EPS)
        ba = ba_ref[b]
        beta = jnp.where(real, jax.nn.sigmoid(ba), 0.0)
        gate = jnp.where(real, -jnp.exp(alog_ref[...]) * jax.nn.softplus(ba + dtb_ref[...]), 0.0)
        betab = lanes_of(beta, 0)
        gb = lanes_of(gate, N_HEADS)
        g1 = gb.astype(BF16)
        r1 = gb - g1.astype(F32)
        g2 = r1.astype(BF16)
        g3 = (r1 - g2.astype(F32)).astype(BF16)
        gs = jnp.concatenate([g1, g2, g3], axis=1)
        q_s[b] = q
        k_s[b] = k
        kb_s[b] = k * betab
        vb_s[b] = v * betab
        gs_s[b] = gs
        gc_s[b] = sum3(_dot(tri_blk, gs))

    def chunk_matrices(q, k, kb, vb, gc, gs):
        dm = sum3(_dot(tri, jnp.where(strict3, gs, jnp.zeros((), BF16))))
        decay = jnp.exp(dm)
        kq = _dot_nt(jnp.concatenate([kb, q], axis=0).astype(BF16), blockdiag(k.astype(BF16)))
        yield
        lmat = jnp.where(strict, kq[0:c] * decay, 0.0)
        attn = jnp.where(incl, kq[c:] * decay, 0.0)
        neg = -lmat
        ph, plo = _split_bf16(neg)
        res = _dot(jnp.concatenate([ph, plo], axis=0), blockdiag(ph))
        res2 = _dot(ph, blockdiag(plo))
        yield
        power = res[0:c] + res[c:] + res2
        tinv = eye + neg
        for step in range(1, 6):
            ph, plo = _split_bf16(power)
            th, tlo = _split_bf16(tinv)
            bh, bl = blockdiag(ph), blockdiag(plo)
            if step < 5:
                res = _dot(jnp.concatenate([ph, plo, th, tlo], axis=0), bh)
                res2 = _dot(jnp.concatenate([ph, th], axis=0), bl)
                yield
                power = res[0:c] + res[c:2 * c] + res2[0:c]
                tinv = tinv + (res[2 * c:3 * c] + res[3 * c:] + res2[c:])
            else:
                res = _dot(jnp.concatenate([th, tlo], axis=0), bh)
                res2 = _dot(th, bl)
                yield
                tinv = tinv + (res[0:c] + res[c:] + res2)
        egc = jnp.exp(gc)
        uw = _dot(tinv.astype(BF16),
                  jnp.concatenate([blockdiag(vb.astype(BF16)), blockdiag((kb * egc).astype(BF16))], axis=1))
        return (uw[:, 0:HW], jnp.concatenate([uw[:, HW:], q * egc], axis=0).astype(BF16), attn.astype(BF16),
                (k * jnp.exp(gc[c - 1:c, :] - gc)).astype(BF16))

    per_trip = 2 if (tb // c) % 2 == 0 else 1

    def matrices(ti, carry):
        chains = [(b, ti * per_trip + n) for n in range(per_trip) for b in range(nb)]
        rows = [pl.ds(pl.multiple_of(ci * c, c), c) for _, ci in chains]
        loaded = [tuple(s[b, r, :] for s in (q_s, k_s, kb_s, vb_s, gc_s, gs_s)) for (b, _), r in zip(chains, rows)]
        results = _lockstep([chunk_matrices(*vals) for vals in loaded])
        for (b, ci), r, (u, wq, attn, kg) in zip(chains, rows, results):
            u_s[b, r, :] = u
            wq_s[b, pl.ds(pl.multiple_of(ci * 2 * c, 2 * c), 2 * c), :] = wq
            a_s[b, r, :] = attn
            kg_s[b, r, :] = kg
        return carry

    lax.fori_loop(0, tb // (c * per_trip), matrices, 0)

    def recur(ci, carry):
        r = pl.ds(pl.multiple_of(ci * c, c), c)
        r2 = pl.ds(pl.multiple_of(ci * 2 * c, 2 * c), 2 * c)
        loaded = [(state_s[b], wq_s[b, r2, :], u_s[b, r, :], a_s[b, r, :], kg_s[b, r, :],
                   gc_s[b, pl.ds(ci * c + c - 1, 1), :]) for b in range(nb)]

        def chunk_step(state, lhs, u, attn, kg, g_last):
            wq = _dot(lhs, state.astype(BF16))
            yield
            v_new = (u - wq[0:c]).astype(BF16)
            av = _dot(attn, blockdiag(v_new))
            upd = _dot_tn(kg, v_new)
            yield
            return wq[c:] + av, state * jnp.exp(g_last) + jnp.where(bd_mask, upd, 0.0)

        results = _lockstep([chunk_step(*vals) for vals in loaded])
        for b, (o, state) in enumerate(results):
            o_s[b, r, :] = o
            state_s[b] = state
        return carry

    lax.fori_loop(0, tb // c, recur, 0)

    gain = gain_ref[...]
    for b in range(nb):
        o = o_s[b]
        zz = z_ref[b]
        on = o * lax.rsqrt(headsum(o * o) * (1.0 / HEAD_DIM) + EPS) * gain
        o_ref[b] = (on * (zz * jax.nn.sigmoid(zz))).astype(o_ref.dtype)


def _deltanet(qkv, z, ba, l, conv_w, alog, dtb, gain):
    bsz, lp, _ = qkv.shape
    tb = _pick_tile(lp, (384, 128))
    halo_blocks = tb // 8
    row = lambda w: pl.BlockSpec((bsz, tb, w), lambda i: (0, i, 0))
    f32s = pltpu.VMEM((bsz, tb, HW), F32)
    b16s = pltpu.VMEM((bsz, tb, HW), BF16)
    return pl.pallas_call(
        functools.partial(_dn_body, tb=tb),
        grid=(lp // tb,),
        in_specs=[row(3 * HW),
                  pl.BlockSpec((bsz, 8, 3 * HW), lambda i: (0, jnp.maximum(i * halo_blocks - 1, 0), 0)),
                  row(HW), row(LANES),
                  _layer_spec(conv_w, l), _layer_spec(alog, l), _layer_spec(dtb, l), _layer_spec(gain, l)],
        out_specs=row(HW),
        out_shape=jax.ShapeDtypeStruct((bsz, lp, HW), BF16),
        scratch_shapes=[f32s, f32s, f32s, f32s,
                        pltpu.VMEM((bsz, tb, 3 * HW), BF16), f32s,
                        f32s, pltpu.VMEM((bsz, 2 * tb, HW), BF16), b16s, b16s,
                        f32s, pltpu.VMEM((bsz, HW, HW), F32)],
        compiler_params=_cparams(("arbitrary",)),
        name="deltanet",
    )(qkv, qkv, z, ba, conv_w, alog, dtb, gain)


def _s5_body(u_ref, mg_ref, wg_ref, vg_ref, ar_ref, ai_ref, d_ref, y_ref, m_ref, w_ref, v_ref, e_s, sp_s, st_s):
    n = u_ref.shape[0] // S5_T
    half = S5_LG * S5_STATE
    t_, cg, p = S5_T, S5_GROUP, S5_STATE

    @pl.when(jnp.logical_and(pl.program_id(1) == 0, pl.program_id(2) == 0))
    def _():
        def placement(rows, cols, col_key, col_group, r):
            rr = lax.broadcasted_iota(jnp.int32, (rows, cols), 0)
            cc = lax.broadcasted_iota(jnp.int32, (rows, cols), 1)
            return jnp.where(jnp.logical_and(rr == col_key(cc), col_group(cc) == r), 1.0, 0.0).astype(BF16)

        for r in range(S5_LG):
            p_out = placement(t_ * cg, t_ * LANES, lambda c: (c // LANES) * cg + c % cg, lambda c: (c // cg) % S5_LG, r)
            p_state = placement(2 * p, 2 * half, lambda c: (c // half) * p + c % p, lambda c: (c // p) % S5_LG, r)
            mr = _dot(mg_ref[r], p_out).astype(BF16)
            wr = _dot(wg_ref[r], p_state).astype(BF16)
            vr = _dot(vg_ref[r], p_out).astype(BF16)
            for tau in range(t_):
                dst = slice(tau * LANES + r * cg, tau * LANES + (r + 1) * cg)
                m_ref[dst, :] = mr[tau * cg:(tau + 1) * cg, :]
                w_ref[dst, :] = wr[tau * cg:(tau + 1) * cg, :]
            for ri in range(2):
                v_ref[ri * half + r * p:ri * half + (r + 1) * p, :] = vr[ri * p:(ri + 1) * p, :]

    @pl.when(pl.program_id(2) == 0)
    def _():
        st_s[...] = jnp.zeros_like(st_s)

    xs = [u_ref[pl.ds(t, n, stride=S5_T), :] for t in range(S5_T)]
    xcat = jnp.concatenate(xs, axis=1).astype(BF16)
    e_s[...] = _dot(xcat, w_ref[...])
    ar = ar_ref[...]
    ai = ai_ref[...]

    def step(ci, s):
        sr, si = s
        row = pl.ds(ci, 1)
        sp_s[row, 0:half] = sr
        sp_s[row, half:] = si
        er = e_s[row, 0:half]
        ei = e_s[row, half:]
        return ar * sr - ai * si + er, ar * si + ai * sr + ei

    sr, si = lax.fori_loop(0, n, step, (st_s[0:1, :], st_s[1:2, :]), unroll=4)
    st_s[0:1, :] = sr
    st_s[1:2, :] = si
    sp = sp_s[...].astype(BF16)
    d = d_ref[...]
    cb = 2 * LANES
    for k in range(S5_T * LANES // cb):
        cols = slice(k * cb, (k + 1) * cb)
        yk = _dot(xcat[:, 0:(k + 1) * cb], m_ref[0:(k + 1) * cb, cols]) + _dot(sp, v_ref[:, cols])
        for t in range(2 * k, 2 * k + 2):
            off = (t - 2 * k) * LANES
            y_ref[pl.ds(t, n, stride=S5_T), :] = jax.nn.gelu(yk[:, off:off + LANES] + d * xs[t])


def _s5(u, l, mats):
    mg, wg, vg, ar, ai, d = mats
    bsz, lp, width = u.shape
    ntile = width // LANES
    rows = _pick_tile(lp, (lp // 2,))
    n = rows // S5_T
    half = S5_LG * S5_STATE
    mat = lambda a: pl.BlockSpec((None, None) + a.shape[2:], lambda j, b, r: (l, j) + (0,) * (a.ndim - 2))
    return pl.pallas_call(
        _s5_body,
        grid=(ntile, bsz, lp // rows),
        in_specs=[pl.BlockSpec((None, rows, LANES), lambda j, b, r: (b, r, j)),
                  mat(mg), mat(wg), mat(vg), mat(ar), mat(ai), mat(d)],
        out_specs=pl.BlockSpec((None, rows, LANES), lambda j, b, r: (b, r, j)),
        out_shape=jax.ShapeDtypeStruct(u.shape, F32),
        scratch_shapes=[pltpu.VMEM((S5_T * LANES, S5_T * LANES), BF16), pltpu.VMEM((S5_T * LANES, 2 * half), BF16),
                        pltpu.VMEM((2 * half, S5_T * LANES), BF16),
                        pltpu.VMEM((n, 2 * half), F32), pltpu.VMEM((n, 2 * half), F32), pltpu.VMEM((8, half), F32)],
        compiler_params=_cparams(("arbitrary", "arbitrary", "arbitrary")),
        name="s5",
    )(u, mg, wg, vg, ar, ai, d)


def _s5_matrices(a_re, a_im, log_dt, b_re, b_im, c_re, c_im, d):
    ng, p = a_re.shape
    t, lg, cg = S5_T, S5_LG, S5_GROUP
    nj = ng // lg
    dt = jnp.exp(log_dt)[:, None]
    lr, li = a_re * dt, a_im * dt
    er = jnp.exp(lr)
    abr, abi = er * jnp.cos(li), er * jnp.sin(li)
    den = a_re * a_re + a_im * a_im
    fr = ((abr - 1.0) * a_re + abi * a_im) / den
    fi = (abi * a_re - (abr - 1.0) * a_im) / den
    bbr = fr[..., None] * b_re - fi[..., None] * b_im
    bbi = fr[..., None] * b_im + fi[..., None] * b_re
    n = jnp.arange(t + 1, dtype=F32)[:, None, None]
    pr = jnp.exp(n * lr[None]) * jnp.cos(n * li[None])
    pi = jnp.exp(n * lr[None]) * jnp.sin(n * li[None])
    car = c_re[None] * pr[:, :, None, :] - c_im[None] * pi[:, :, None, :]
    cai = c_re[None] * pi[:, :, None, :] + c_im[None] * pr[:, :, None, :]
    kern = jnp.einsum('ngop,gpi->ngio', car[:t], bbr) - jnp.einsum('ngop,gpi->ngio', cai[:t], bbi)
    lag = jnp.arange(t)[None, :, None] - jnp.arange(t)[:, None, None]
    select = (lag == jnp.arange(t)[None, None, :]).astype(F32)
    mg = jnp.einsum('aun,ngio->gaiuo', select, kern, precision=lax.Precision.HIGHEST)
    mg = mg.reshape(nj, lg, t * cg, t * cg).astype(BF16)
    nrev = (t - 1) - jnp.arange(t, dtype=F32)[:, None, None]
    rev_r = (jnp.exp(nrev * lr[None]) * jnp.cos(nrev * li[None]))[..., None]
    rev_i = (jnp.exp(nrev * lr[None]) * jnp.sin(nrev * li[None]))[..., None]
    wg = jnp.stack([rev_r * bbr[None] - rev_i * bbi[None], rev_r * bbi[None] + rev_i * bbr[None]])
    wg = jnp.transpose(wg, (2, 1, 4, 0, 3)).reshape(nj, lg, t * cg, 2 * p).astype(BF16)
    vg = jnp.stack([car[1:], -cai[1:]])
    vg = jnp.transpose(vg, (2, 0, 4, 1, 3)).reshape(nj, lg, 2 * p, t * cg).astype(BF16)

    ar = pr[t].reshape(nj, 1, lg * p)
    ai = pi[t].reshape(nj, 1, lg * p)
    return mg, wg, vg, ar, ai, d.reshape(nj, 1, LANES)


def _inproj_weight(w_in):
    pad = jnp.zeros(w_in.shape[:-1] + (LANES - 2 * N_HEADS,), w_in.dtype)
    return jnp.concatenate([w_in[..., 0:1800], pad, w_in[..., 1800:2312]], axis=-1).astype(BF16)


def kernel(x, meta_tokens, ffn1_norm, ffn1_w_gate, ffn1_w_up, ffn1_w_down, mix_norm, w_in, sb_out_norm, dn_conv_w, dn_a_log, dn_dt_bias, dn_out_norm, s5_a_re, s5_a_im, s5_log_dt, s5_b_re, s5_b_im, s5_c_re, s5_c_im, s5_d, s5_w_glu, s5_b_glu, s5_out_norm, w_out, ffn2_norm, ffn2_w_gate, ffn2_w_up, ffn2_w_down, final_norm):
    bsz, seq, d = x.shape
    depth = w_in.shape[0]
    lp = FRONT_PAD + N_META + seq
    ntok = bsz * lp
    assert lp % SB_BLOCK == 0 and lp % DN_CHUNK == 0 and lp % (2 * S5_T) == 0
    meta = jnp.broadcast_to(meta_tokens[None].astype(x.dtype), (bsz, N_META, d))
    h = jnp.concatenate([jnp.zeros((bsz, FRONT_PAD, d), x.dtype), meta, x], axis=1).reshape(ntok, d)
    tok3 = lambda a: a.reshape(bsz, lp, a.shape[-1])
    bf = lambda a: a.astype(BF16)
    vec = lambda a: a.astype(F32)[:, None, :]
    lane = lambda a: jnp.zeros((depth, 1, LANES), F32).at[:, 0, N_HEADS:2 * N_HEADS].set(a.astype(F32))
    ffn1 = (vec(ffn1_norm), bf(ffn1_w_gate), bf(ffn1_w_up), bf(ffn1_w_down), vec(mix_norm), _inproj_weight(w_in))
    sb_gain = vec(sb_out_norm)
    dn = (dn_conv_w.astype(F32), lane(dn_a_log), lane(dn_dt_bias), vec(jnp.tile(dn_out_norm, (1, N_HEADS))))
    s5_mats = jax.vmap(_s5_matrices)(s5_a_re, s5_a_im, s5_log_dt, s5_b_re, s5_b_im, s5_c_re, s5_c_im, s5_d)
    tail = (bf(s5_w_glu), vec(s5_b_glu), vec(s5_out_norm), bf(w_out),
            vec(ffn2_norm), bf(ffn2_w_gate), bf(ffn2_w_up), bf(ffn2_w_down))
    for l in range(depth):
        h, sbq, sbk, sbv, dqkv, dz, dba, u5 = _ffn_inproj(h, l, *ffn1)
        o_sb = _sb_attention(tok3(sbq), tok3(sbk), tok3(sbv), l, sb_gain)
        o_dn = _deltanet(tok3(dqkv), tok3(dz), tok3(dba), l, *dn)
        y5 = _s5(tok3(u5), l, s5_mats)
        branches = (o_sb.reshape(ntok, HW), o_dn.reshape(ntok, HW), y5.reshape(ntok, -1))
        if l < depth - 1:
            h = _mixout_ffn(h, *branches, l, *tail)
    out = _mixout_ffn_final(h, *branches, depth - 1, *tail, final_norm[None], bsz, FRONT_PAD + N_META)
    return out.reshape(bsz, seq, d)
```

```python
import functools

import jax
import jax.numpy as jnp
from jax import lax
from jax.experimental import pallas as pl
from jax.experimental.pallas import tpu as pltpu

F32 = jnp.float32
BF16 = jnp.bfloat16

N_META = 16
HEAD_DIM = 64
N_HEADS = 4
HW = N_HEADS * HEAD_DIM
SB_BLOCK = 128
SB_WINDOW = 256
DN_CHUNK = 64
DN_CONV = 4
FF_CHUNK = 256
LANES = 128
S5_GROUP = 16
S5_STATE = 64
S5_T = 16
S5_LG = LANES // S5_GROUP
EPS = 1e-6
FRONT_PAD = (-N_META) % SB_BLOCK
EXP_UNDERFLOW = -88.0
VMEM_LIMIT = 56 * 1024 * 1024


def _cparams(sem):
    return pltpu.CompilerParams(dimension_semantics=sem, vmem_limit_bytes=VMEM_LIMIT)


def _pick_tile(n, candidates):
    for c in candidates:
        if n % c == 0:
            return c
    raise ValueError(f"no tile for {n}")


def _rms(x):
    return x * lax.rsqrt(jnp.mean(x * x, axis=-1, keepdims=True) + EPS)


def _split_bf16(x):
    hi = x.astype(BF16)
    lo = (x - hi.astype(F32)).astype(BF16)
    return hi, lo


def _dot(a, b):
    return jnp.dot(a, b, preferred_element_type=F32)


def _dot_nt(a, b):
    return lax.dot_general(a, b, (((1,), (1,)), ((), ())), preferred_element_type=F32)


def _dot_tn(a, b):
    return lax.dot_general(a, b, (((0,), (0,)), ((), ())), preferred_element_type=F32)


def _dot_split(x, m_bf16):
    hi, lo = _split_bf16(x)
    return _dot(hi, m_bf16) + _dot(lo, m_bf16)


def _lockstep(gens):
    results = [None] * len(gens)
    live = list(range(len(gens)))
    while live:
        still = []
        for n in live:
            try:
                next(gens[n])
                still.append(n)
            except StopIteration as stop:
                results[n] = stop.value
        live = still
    return results


def _const_spec(shape):
    nd = len(shape)
    return pl.BlockSpec(shape, lambda *_: (0,) * nd, pipeline_mode=pl.Buffered(1))


def _layer_spec(a, l):
    nd = a.ndim - 1
    return pl.BlockSpec((None,) + a.shape[1:], lambda *_: (l,) + (0,) * nd, pipeline_mode=pl.Buffered(1))


def _swiglu_half_step(x, g_ref, wg_ref, wu_ref, wd_ref, acc_ref):
    xn = (_rms(x) * g_ref[...]).astype(BF16)

    def chunk_out(c):
        cols = pl.ds(pl.multiple_of(c * FF_CHUNK, FF_CHUNK), FF_CHUNK)
        gate = _dot(xn, wg_ref[:, cols])
        up = _dot(xn, wu_ref[:, cols])
        act = (gate * jax.nn.sigmoid(gate) * up).astype(BF16)
        return _dot(act, wd_ref[cols, :])

    acc_ref[...] = chunk_out(0)

    def chunk(c, carry):
        acc_ref[...] += chunk_out(c)
        return carry

    lax.fori_loop(1, wg_ref.shape[1] // FF_CHUNK, chunk, 0, unroll=5)
    return x + 0.5 * acc_ref[...]


_C_SBQ, _C_SBK, _C_SBV, _C_DNQKV, _C_DNZ, _C_DNBA, _C_S5U, _C_END = 0, 256, 512, 768, 1536, 1792, 1920, 2432
_PROJ_WIDTHS = (HW, HW, HW, 3 * HW, HW, LANES, 512)
_PROJ_DTYPES = (BF16, BF16, BF16, F32, F32, F32, F32)


def _ffn_inproj_body(h_ref, g_ref, wg_ref, wu_ref, wd_ref, gm_ref, w_ref,
                     o_ref, q_ref, k_ref, v_ref, dqkv_ref, dz_ref, dba_ref, u_ref, acc_ref):
    h = _swiglu_half_step(h_ref[...], g_ref, wg_ref, wu_ref, wd_ref, acc_ref)
    o_ref[...] = h
    xn = (_rms(h) * gm_ref[...]).astype(BF16)
    q_ref[...] = (_dot(xn, w_ref[:, _C_SBQ:_C_SBK]) * (HEAD_DIM ** -0.5)).astype(BF16)
    k_ref[...] = _dot(xn, w_ref[:, _C_SBK:_C_SBV]).astype(BF16)
    v_ref[...] = _dot(xn, w_ref[:, _C_SBV:_C_DNQKV]).astype(BF16)
    dqkv_ref[...] = _dot(xn, w_ref[:, _C_DNQKV:_C_DNZ])
    dz_ref[...] = _dot(xn, w_ref[:, _C_DNZ:_C_DNBA])
    dba_ref[...] = _dot(xn, w_ref[:, _C_DNBA:_C_S5U])
    u_ref[...] = _dot(xn, w_ref[:, _C_S5U:_C_END])


def _ffn_inproj(h, l, g, wg, wu, wd, gm, w):
    ntok, d = h.shape
    tm = _pick_tile(ntok, (768, 512, 384, 256, 128))
    row = lambda wd_: pl.BlockSpec((tm, wd_), lambda i: (i, 0))
    return pl.pallas_call(
        _ffn_inproj_body,
        grid=(ntok // tm,),
        in_specs=[row(d)] + [_layer_spec(a, l) for a in (g, wg, wu, wd, gm, w)],
        out_specs=[row(d)] + [row(wd_) for wd_ in _PROJ_WIDTHS],
        out_shape=[jax.ShapeDtypeStruct((ntok, d), F32)]
        + [jax.ShapeDtypeStruct((ntok, wd_), dt) for wd_, dt in zip(_PROJ_WIDTHS, _PROJ_DTYPES)],
        scratch_shapes=[pltpu.VMEM((tm, d), F32)],
        compiler_params=_cparams(("parallel",)),
        name="ffn_inproj",
    )(h, g, wg, wu, wd, gm, w)


def _mixout_ffn_body(h_ref, osb_ref, odn_ref, y_ref, wglu_ref, bglu_ref, g5_ref, wout_ref,
                     g_ref, wg_ref, wu_ref, wd_ref, *rest, final):
    if final:
        gf_ref, o_ref, acc_ref = rest
    else:
        o_ref, acc_ref = rest
    y = y_ref[...]
    gate = _dot(y.astype(BF16), wglu_ref[...]) + bglu_ref[...]
    o5 = (_rms(y * jax.nn.sigmoid(gate)) * g5_ref[...]).astype(BF16)
    mixed = (_dot(osb_ref[...], wout_ref[0:HW, :]) + _dot(odn_ref[...], wout_ref[HW:2 * HW, :])
             + _dot(o5, wout_ref[2 * HW:, :]))
    h = _swiglu_half_step(h_ref[...] + mixed, g_ref, wg_ref, wu_ref, wd_ref, acc_ref)
    o_ref[...] = _rms(h) * gf_ref[...] if final else h


def _mixout_ffn(h, osb, odn, y5, l, wglu, bglu, g5, wout, g, wg, wu, wd):
    ntok, d = h.shape
    tm = _pick_tile(ntok, (768, 512, 384, 256, 128))
    row = lambda w: pl.BlockSpec((tm, w), lambda i: (i, 0))
    consts = [wglu, bglu, g5, wout, g, wg, wu, wd]
    return pl.pallas_call(
        functools.partial(_mixout_ffn_body, final=False),
        grid=(ntok // tm,),
        in_specs=[row(d), row(HW), row(HW), row(512)] + [_layer_spec(c, l) for c in consts],
        out_specs=row(d),
        out_shape=jax.ShapeDtypeStruct((ntok, d), F32),
        scratch_shapes=[pltpu.VMEM((tm, d), F32)],
        compiler_params=_cparams(("parallel",)),
        name="mixout_ffn",
    )(h, osb, odn, y5, *consts)


def _mixout_ffn_final(h, osb, odn, y5, l, wglu, bglu, g5, wout, g, wg, wu, wd, gf, bsz, skip):
    ntok, d = h.shape
    lp = ntok // bsz
    seq = lp - skip
    tm = _pick_tile(seq, (1024, 512, 256, 128))
    per = seq // tm
    sub = 8
    assert lp % sub == 0 and skip % sub == 0
    row = lambda w: pl.BlockSpec(
        (pl.Element(tm), pl.Element(w)),
        lambda b, i: ((b * (lp // sub) + skip // sub + i * (tm // sub)) * sub, 0))
    consts = [wglu, bglu, g5, wout, g, wg, wu, wd]
    return pl.pallas_call(
        functools.partial(_mixout_ffn_body, final=True),
        grid=(bsz, per),
        in_specs=[row(d), row(HW), row(HW), row(512)] + [_layer_spec(c, l) for c in consts]
        + [_const_spec(gf.shape)],
        out_specs=pl.BlockSpec((tm, d), lambda b, i: (b * per + i, 0)),
        out_shape=jax.ShapeDtypeStruct((bsz * seq, d), F32),
        scratch_shapes=[pltpu.VMEM((tm, d), F32)],
        compiler_params=_cparams(("parallel", "parallel")),
        name="mixout_ffn_final",
    )(h, osb, odn, y5, *consts, gf)


def _sb_body(q_ref, k_ref, v_ref, sfx_ref, g_ref, o_ref):
    i = pl.program_id(0)
    nb = q_ref.shape[0]
    tq, tk = SB_BLOCK, SB_WINDOW
    row = lax.broadcasted_iota(jnp.int32, (tq, tk), 0)
    col = lax.broadcasted_iota(jnp.int32, (tq, tk), 1)
    qpos = i * tq + row
    chains = [(b, h) for b in range(nb) for h in range(N_HEADS)]
    qs = [q_ref[b, :, h * HEAD_DIM:(h + 1) * HEAD_DIM] for b, h in chains]
    sfx = sfx_ref[...]

    def cond(state):
        w0, done = state[0], state[1]
        return jnp.logical_and(w0 + tk > 0, jnp.logical_not(done))

    def body(state):
        w0, _, carry, acc = state
        start = pl.multiple_of(jnp.maximum(w0, 0), tq)
        kpos = start + col
        valid = jnp.logical_and(kpos < jnp.minimum(qpos, w0 + tk), kpos >= FRONT_PAD)

        def chain(n, b, h):
            hs = slice(h * HEAD_DIM, (h + 1) * HEAD_DIM)
            z = _dot_nt(qs[n], k_ref[b, pl.ds(start, tk), hs])
            yield
            sp = jnp.maximum(z, 0.0) + jnp.log(1.0 + jnp.exp(-jnp.abs(z)))
            lk = jnp.where(valid, -sp, 0.0)
            sums = _dot_split(lk, sfx)
            yield
            c = carry[n]
            expo = z - sp + sums[:, 0:tk] + jnp.concatenate([c] * (tk // LANES), axis=1)
            w = jnp.where(valid, jnp.exp(expo), 0.0)
            pv = _dot(w.astype(BF16), v_ref[b, pl.ds(start, tk), hs])
            yield
            return c + sums[:, tk:], acc[n] + pv

        results = _lockstep([chain(n, b, h) for n, (b, h) in enumerate(chains)])
        new_carry = tuple(r[0] for r in results)
        top = functools.reduce(jnp.maximum, new_carry)
        return w0 - tk, jnp.max(top) < EXP_UNDERFLOW, new_carry, tuple(r[1] for r in results)

    zc = tuple(jnp.zeros((tq, LANES), F32) for _ in chains)
    za = tuple(jnp.zeros((tq, HEAD_DIM), F32) for _ in chains)
    _, _, _, acc = lax.while_loop(cond, body, ((i + 1) * tq - tk, jnp.bool_(False), zc, za))
    gain = g_ref[...]
    for b in range(nb):
        o_ref[b] = jnp.concatenate([_rms(acc[b * N_HEADS + h]) * gain for h in range(N_HEADS)],
                                   axis=1).astype(o_ref.dtype)


def _sb_attention(q, k, v, l, gain):
    bsz, lp, _ = q.shape
    nblk = lp // SB_BLOCK
    tk = SB_WINDOW
    r = jnp.arange(tk)
    sfx = jnp.concatenate([(r[:, None] > r[None, :]), jnp.ones((tk, LANES), bool)], axis=1).astype(BF16)
    full = pl.BlockSpec((bsz, lp, HW), lambda i: (0, 0, 0), pipeline_mode=pl.Buffered(1))
    return pl.pallas_call(
        _sb_body,
        grid=(nblk,),
        in_specs=[pl.BlockSpec((bsz, SB_BLOCK, HW), lambda i: (0, i, 0)), full, full,
                  _const_spec(sfx.shape), _layer_spec(gain, l)],
        out_specs=pl.BlockSpec((bsz, SB_BLOCK, HW), lambda i: (0, i, 0)),
        out_shape=jax.ShapeDtypeStruct((bsz, lp, HW), BF16),
        compiler_params=_cparams(("parallel",)),
        name="sb_attention",
    )(q, k, v, sfx, gain)


def _head_mask(rows, cols, rdiv, cdiv):
    r = lax.broadcasted_iota(jnp.int32, (rows, cols), 0) // rdiv
    c = lax.broadcasted_iota(jnp.int32, (rows, cols), 1) // cdiv
    return r == c


def _dn_body(qkv_ref, halo_ref, z_ref, ba_ref, cw_ref, alog_ref, dtb_ref, gain_ref, o_ref,
             q_s, k_s, kb_s, vb_s, gs_s, gc_s, u_s, wq_s, a_s, kg_s, o_s, state_s, *, tb):
    i = pl.program_id(0)
    nb = qkv_ref.shape[0]
    c = DN_CHUNK

    @pl.when(i == 0)
    def _():
        state_s[...] = jnp.zeros_like(state_s)

    bd_mask = _head_mask(HW, HW, HEAD_DIM, HEAD_DIM)
    ones_bd = jnp.where(bd_mask, 1.0, 0.0).astype(BF16)
    rowi = lax.broadcasted_iota(jnp.int32, (c, HW), 0)
    colj = lax.broadcasted_iota(jnp.int32, (c, HW), 1) % HEAD_DIM
    strict = rowi > colj
    incl = rowi >= colj
    eye = jnp.where(rowi == colj, 1.0, 0.0)
    strict3 = jnp.concatenate([strict] * 3, axis=1)
    tri = jnp.where(lax.broadcasted_iota(jnp.int32, (c, c), 0) >= lax.broadcasted_iota(jnp.int32, (c, c), 1),
                    1.0, 0.0).astype(BF16)
    rt = lax.broadcasted_iota(jnp.int32, (tb, tb), 0)
    ct = lax.broadcasted_iota(jnp.int32, (tb, tb), 1)
    tri_blk = jnp.where(jnp.logical_and(rt // c == ct // c, rt >= ct), 1.0, 0.0).astype(BF16)

    def headsum(x):
        return _dot_split(x, ones_bd)

    def blockdiag(y):
        return jnp.where(bd_mask, jnp.concatenate([y] * N_HEADS, axis=0), jnp.zeros((), y.dtype))

    def lanes_of(x, first):
        return jnp.concatenate(
            [jnp.broadcast_to(x[:, first + h:first + h + 1], (tb, HEAD_DIM)) for h in range(N_HEADS)], axis=1)

    def sum3(x):
        return x[:, 0:HW] + x[:, HW:2 * HW] + x[:, 2 * HW:3 * HW]

    cw = cw_ref[...]
    real = (i * tb + lax.broadcasted_iota(jnp.int32, (tb, 1), 0)) >= FRONT_PAD
    for b in range(nb):
        x = qkv_ref[b]
        halo = jnp.where(i > 0, halo_ref[b], 0.0)
        xx = jnp.concatenate([halo, x], axis=0)
        conv = (cw[3:4] * x + cw[2:3] * xx[7:7 + tb] + cw[1:2] * xx[6:6 + tb] + cw[0:1] * xx[5:5 + tb])
        y = conv * jax.nn.sigmoid(conv)
        qr, kr, v = y[:, 0:HW], y[:, HW:2 * HW], y[:, 2 * HW:3 * HW]
        q = qr * lax.rsqrt(headsum(qr * qr) + EPS) * (HEAD_DIM ** -0.5)
        k = kr * lax.rsqrt(headsum(kr * kr) + EPS)
        ba = ba_ref[b]
        beta = jnp.where(real, jax.nn.sigmoid(ba), 0.0)
        gate = jnp.where(real, -jnp.exp(alog_ref[...]) * jax.nn.softplus(ba + dtb_ref[...]), 0.0)
        betab = lanes_of(beta, 0)
        gb = lanes_of(gate, N_HEADS)
        g1 = gb.astype(BF16)
        r1 = gb - g1.astype(F32)
        g2 = r1.astype(BF16)
        g3 = (r1 - g2.astype(F32)).astype(BF16)
        gs = jnp.concatenate([g1, g2, g3], axis=1)
        q_s[b] = q
        k_s[b] = k
        kb_s[b] = k * betab
        vb_s[b] = v * betab
        gs_s[b] = gs
        gc_s[b] = sum3(_dot(tri_blk, gs))

    def chunk_matrices(q, k, kb, vb, gc, gs):
        dm = sum3(_dot(tri, jnp.where(strict3, gs, jnp.zeros((), BF16))))
        decay = jnp.exp(dm)
        kq = _dot_nt(jnp.concatenate([kb, q], axis=0).astype(BF16), blockdiag(k.astype(BF16)))
        yield
        lmat = jnp.where(strict, kq[0:c] * decay, 0.0)
        attn = jnp.where(incl, kq[c:] * decay, 0.0)
        neg = -lmat
        ph, plo = _split_bf16(neg)
        res = _dot(jnp.concatenate([ph, plo], axis=0), blockdiag(ph))
        res2 = _dot(ph, blockdiag(plo))
        yield
        power = res[0:c] + res[c:] + res2
        tinv = eye + neg
        for step in range(1, 6):
            ph, plo = _split_bf16(power)
            th, tlo = _split_bf16(tinv)
            bh, bl = blockdiag(ph), blockdiag(plo)
            if step < 5:
                res = _dot(jnp.concatenate([ph, plo, th, tlo], axis=0), bh)
                res2 = _dot(jnp.concatenate([ph, th], axis=0), bl)
                yield
                power = res[0:c] + res[c:2 * c] + res2[0:c]
                tinv = tinv + (res[2 * c:3 * c] + res[3 * c:] + res2[c:])
            else:
                res = _dot(jnp.concatenate([th, tlo], axis=0), bh)
                res2 = _dot(th, bl)
                yield
                tinv = tinv + (res[0:c] + res[c:] + res2)
        egc = jnp.exp(gc)
        uw = _dot(tinv.astype(BF16),
                  jnp.concatenate([blockdiag(vb.astype(BF16)), blockdiag((kb * egc).astype(BF16))], axis=1))
        return (uw[:, 0:HW], jnp.concatenate([uw[:, HW:], q * egc], axis=0).astype(BF16), attn.astype(BF16),
                (k * jnp.exp(gc[c - 1:c, :] - gc)).astype(BF16))

    per_trip = 2 if (tb // c) % 2 == 0 else 1

    def matrices(ti, carry):
        chains = [(b, ti * per_trip + n) for n in range(per_trip) for b in range(nb)]
        rows = [pl.ds(pl.multiple_of(ci * c, c), c) for _, ci in chains]
        loaded = [tuple(s[b, r, :] for s in (q_s, k_s, kb_s, vb_s, gc_s, gs_s)) for (b, _), r in zip(chains, rows)]
        results = _lockstep([chunk_matrices(*vals) for vals in loaded])
        for (b, ci), r, (u, wq, attn, kg) in zip(chains, rows, results):
            u_s[b, r, :] = u
            wq_s[b, pl.ds(pl.multiple_of(ci * 2 * c, 2 * c), 2 * c), :] = wq
            a_s[b, r, :] = attn
            kg_s[b, r, :] = kg
        return carry

    lax.fori_loop(0, tb // (c * per_trip), matrices, 0)

    def recur(ci, carry):
        r = pl.ds(pl.multiple_of(ci * c, c), c)
        r2 = pl.ds(pl.multiple_of(ci * 2 * c, 2 * c), 2 * c)
        loaded = [(state_s[b], wq_s[b, r2, :], u_s[b, r, :], a_s[b, r, :], kg_s[b, r, :],
                   gc_s[b, pl.ds(ci * c + c - 1, 1), :]) for b in range(nb)]

        def chunk_step(state, lhs, u, attn, kg, g_last):
            wq = _dot(lhs, state.astype(BF16))
            yield
            v_new = (u - wq[0:c]).astype(BF16)
            av = _dot(attn, blockdiag(v_new))
            upd = _dot_tn(kg, v_new)
            yield
            return wq[c:] + av, state * jnp.exp(g_last) + jnp.where(bd_mask, upd, 0.0)

        results = _lockstep([chunk_step(*vals) for vals in loaded])
        for b, (o, state) in enumerate(results):
            o_s[b, r, :] = o
            state_s[b] = state
        return carry

    lax.fori_loop(0, tb // c, recur, 0, unroll=2)

    gain = gain_ref[...]
    for b in range(nb):
        o = o_s[b]
        zz = z_ref[b]
        on = o * lax.rsqrt(headsum(o * o) * (1.0 / HEAD_DIM) + EPS) * gain
        o_ref[b] = (on * (zz * jax.nn.sigmoid(zz))).astype(o_ref.dtype)


def _deltanet(qkv, z, ba, l, conv_w, alog, dtb, gain):
    bsz, lp, _ = qkv.shape
    tb = _pick_tile(lp, (384, 128))
    halo_blocks = tb // 8
    row = lambda w: pl.BlockSpec((bsz, tb, w), lambda i: (0, i, 0))
    f32s = pltpu.VMEM((bsz, tb, HW), F32)
    b16s = pltpu.VMEM((bsz, tb, HW), BF16)
    return pl.pallas_call(
        functools.partial(_dn_body, tb=tb),
        grid=(lp // tb,),
        in_specs=[row(3 * HW),
                  pl.BlockSpec((bsz, 8, 3 * HW), lambda i: (0, jnp.maximum(i * halo_blocks - 1, 0), 0)),
                  row(HW), row(LANES),
                  _layer_spec(conv_w, l), _layer_spec(alog, l), _layer_spec(dtb, l), _layer_spec(gain, l)],
        out_specs=row(HW),
        out_shape=jax.ShapeDtypeStruct((bsz, lp, HW), BF16),
        scratch_shapes=[f32s, f32s, f32s, f32s,
                        pltpu.VMEM((bsz, tb, 3 * HW), BF16), f32s,
                        f32s, pltpu.VMEM((bsz, 2 * tb, HW), BF16), b16s, b16s,
                        f32s, pltpu.VMEM((bsz, HW, HW), F32)],
        compiler_params=_cparams(("arbitrary",)),
        name="deltanet",
    )(qkv, qkv, z, ba, conv_w, alog, dtb, gain)


def _s5_body(u_ref, mg_ref, wg_ref, vg_ref, ar_ref, ai_ref, d_ref, y_ref, m_ref, w_ref, v_ref, e_s, sp_s, st_s):
    n = u_ref.shape[0] // S5_T
    half = S5_LG * S5_STATE
    t_, cg, p = S5_T, S5_GROUP, S5_STATE

    @pl.when(jnp.logical_and(pl.program_id(1) == 0, pl.program_id(2) == 0))
    def _():
        def placement(rows, cols, col_key, col_group, r):
            rr = lax.broadcasted_iota(jnp.int32, (rows, cols), 0)
            cc = lax.broadcasted_iota(jnp.int32, (rows, cols), 1)
            return jnp.where(jnp.logical_and(rr == col_key(cc), col_group(cc) == r), 1.0, 0.0).astype(BF16)

        for r in range(S5_LG):
            p_out = placement(t_ * cg, t_ * LANES, lambda c: (c // LANES) * cg + c % cg, lambda c: (c // cg) % S5_LG, r)
            p_state = placement(2 * p, 2 * half, lambda c: (c // half) * p + c % p, lambda c: (c // p) % S5_LG, r)
            mr = _dot(mg_ref[r], p_out).astype(BF16)
            wr = _dot(wg_ref[r], p_state).astype(BF16)
            vr = _dot(vg_ref[r], p_out).astype(BF16)
            for tau in range(t_):
                dst = slice(tau * LANES + r * cg, tau * LANES + (r + 1) * cg)
                m_ref[dst, :] = mr[tau * cg:(tau + 1) * cg, :]
                w_ref[dst, :] = wr[tau * cg:(tau + 1) * cg, :]
            for ri in range(2):
                v_ref[ri * half + r * p:ri * half + (r + 1) * p, :] = vr[ri * p:(ri + 1) * p, :]

    @pl.when(pl.program_id(2) == 0)
    def _():
        st_s[...] = jnp.zeros_like(st_s)

    xs = [u_ref[pl.ds(t, n, stride=S5_T), :] for t in range(S5_T)]
    xcat = jnp.concatenate(xs, axis=1).astype(BF16)
    e_s[...] = _dot(xcat, w_ref[...])
    ar = ar_ref[...]
    ai = ai_ref[...]

    def step(ci, s):
        sr, si = s
        row = pl.ds(ci, 1)
        sp_s[row, 0:half] = sr
        sp_s[row, half:] = si
        er = e_s[row, 0:half]
        ei = e_s[row, half:]
        return ar * sr - ai * si + er, ar * si + ai * sr + ei

    sr, si = lax.fori_loop(0, n, step, (st_s[0:1, :], st_s[1:2, :]), unroll=4)
    st_s[0:1, :] = sr
    st_s[1:2, :] = si
    sp = sp_s[...].astype(BF16)
    d = d_ref[...]
    cb = 2 * LANES
    for k in range(S5_T * LANES // cb):
        cols = slice(k * cb, (k + 1) * cb)
        yk = _dot(xcat[:, 0:(k + 1) * cb], m_ref[0:(k + 1) * cb, cols]) + _dot(sp, v_ref[:, cols])
        for t in range(2 * k, 2 * k + 2):
            off = (t - 2 * k) * LANES
            y_ref[pl.ds(t, n, stride=S5_T), :] = jax.nn.gelu(yk[:, off:off + LANES] + d * xs[t])


def _s5(u, l, mats):
    mg, wg, vg, ar, ai, d = mats
    bsz, lp, width = u.shape
    ntile = width // LANES
    rows = _pick_tile(lp, (lp // 2,))
    n = rows // S5_T
    half = S5_LG * S5_STATE
    mat = lambda a: pl.BlockSpec((None, None) + a.shape[2:], lambda j, b, r: (l, j) + (0,) * (a.ndim - 2))
    return pl.pallas_call(
        _s5_body,
        grid=(ntile, bsz, lp // rows),
        in_specs=[pl.BlockSpec((None, rows, LANES), lambda j, b, r: (b, r, j)),
                  mat(mg), mat(wg), mat(vg), mat(ar), mat(ai), mat(d)],
        out_specs=pl.BlockSpec((None, rows, LANES), lambda j, b, r: (b, r, j)),
        out_shape=jax.ShapeDtypeStruct(u.shape, F32),
        scratch_shapes=[pltpu.VMEM((S5_T * LANES, S5_T * LANES), BF16), pltpu.VMEM((S5_T * LANES, 2 * half), BF16),
                        pltpu.VMEM((2 * half, S5_T * LANES), BF16),
                        pltpu.VMEM((n, 2 * half), F32), pltpu.VMEM((n, 2 * half), F32), pltpu.VMEM((8, half), F32)],
        compiler_params=_cparams(("arbitrary", "arbitrary", "arbitrary")),
        name="s5",
    )(u, mg, wg, vg, ar, ai, d)


def _s5_matrices(a_re, a_im, log_dt, b_re, b_im, c_re, c_im, d):
    ng, p = a_re.shape
    t, lg, cg = S5_T, S5_LG, S5_GROUP
    nj = ng // lg
    dt = jnp.exp(log_dt)[:, None]
    lr, li = a_re * dt, a_im * dt
    er = jnp.exp(lr)
    abr, abi = er * jnp.cos(li), er * jnp.sin(li)
    den = a_re * a_re + a_im * a_im
    fr = ((abr - 1.0) * a_re + abi * a_im) / den
    fi = (abi * a_re - (abr - 1.0) * a_im) / den
    bbr = fr[..., None] * b_re - fi[..., None] * b_im
    bbi = fr[..., None] * b_im + fi[..., None] * b_re
    n = jnp.arange(t + 1, dtype=F32)[:, None, None]
    pr = jnp.exp(n * lr[None]) * jnp.cos(n * li[None])
    pi = jnp.exp(n * lr[None]) * jnp.sin(n * li[None])
    car = c_re[None] * pr[:, :, None, :] - c_im[None] * pi[:, :, None, :]
    cai = c_re[None] * pi[:, :, None, :] + c_im[None] * pr[:, :, None, :]
    kern = jnp.einsum('ngop,gpi->ngio', car[:t], bbr) - jnp.einsum('ngop,gpi->ngio', cai[:t], bbi)
    rows = lax.broadcasted_iota(jnp.int32, (t * cg, t * t * cg), 0)
    cols = lax.broadcasted_iota(jnp.int32, (t * cg, t * t * cg), 1)
    toeplitz = jnp.logical_and((cols // cg) % t - cols // (t * cg) == rows // cg, cols % cg == rows % cg)
    k3 = jnp.transpose(kern, (1, 2, 0, 3)).reshape(ng * cg, t * cg).astype(BF16)
    mg = jnp.dot(k3, toeplitz.astype(BF16), preferred_element_type=BF16)
    mg = jnp.transpose(mg.reshape(ng, cg, t, t * cg), (0, 2, 1, 3)).reshape(nj, lg, t * cg, t * cg)
    nrev = (t - 1) - jnp.arange(t, dtype=F32)[:, None, None]
    rev_r = (jnp.exp(nrev * lr[None]) * jnp.cos(nrev * li[None]))[..., None]
    rev_i = (jnp.exp(nrev * lr[None]) * jnp.sin(nrev * li[None]))[..., None]
    wg = jnp.stack([rev_r * bbr[None] - rev_i * bbi[None], rev_r * bbi[None] + rev_i * bbr[None]])
    wg = jnp.transpose(wg, (2, 1, 4, 0, 3)).reshape(nj, lg, t * cg, 2 * p).astype(BF16)
    vg = jnp.stack([car[1:], -cai[1:]])
    vg = jnp.transpose(vg, (2, 0, 4, 1, 3)).reshape(nj, lg, 2 * p, t * cg).astype(BF16)

    ar = pr[t].reshape(nj, 1, lg * p)
    ai = pi[t].reshape(nj, 1, lg * p)
    return mg, wg, vg, ar, ai, d.reshape(nj, 1, LANES)


def _inproj_weight(w_in):
    pad = jnp.zeros(w_in.shape[:-1] + (LANES - 2 * N_HEADS,), w_in.dtype)
    return jnp.concatenate([w_in[..., 0:1800], pad, w_in[..., 1800:2312]], axis=-1).astype(BF16)


def kernel(x, meta_tokens, ffn1_norm, ffn1_w_gate, ffn1_w_up, ffn1_w_down, mix_norm, w_in, sb_out_norm, dn_conv_w, dn_a_log, dn_dt_bias, dn_out_norm, s5_a_re, s5_a_im, s5_log_dt, s5_b_re, s5_b_im, s5_c_re, s5_c_im, s5_d, s5_w_glu, s5_b_glu, s5_out_norm, w_out, ffn2_norm, ffn2_w_gate, ffn2_w_up, ffn2_w_down, final_norm):
    bsz, seq, d = x.shape
    depth = w_in.shape[0]
    lp = FRONT_PAD + N_META + seq
    ntok = bsz * lp
    assert lp % SB_BLOCK == 0 and lp % DN_CHUNK == 0 and lp % (2 * S5_T) == 0
    meta = jnp.broadcast_to(meta_tokens[None].astype(x.dtype), (bsz, N_META, d))
    h = jnp.concatenate([jnp.zeros((bsz, FRONT_PAD, d), x.dtype), meta, x], axis=1).reshape(ntok, d)
    tok3 = lambda a: a.reshape(bsz, lp, a.shape[-1])
    bf = lambda a: a.astype(BF16)
    vec = lambda a: a.astype(F32)[:, None, :]
    lane = lambda a: jnp.zeros((depth, 1, LANES), F32).at[:, 0, N_HEADS:2 * N_HEADS].set(a.astype(F32))
    ffn1 = (vec(ffn1_norm), bf(ffn1_w_gate), bf(ffn1_w_up), bf(ffn1_w_down), vec(mix_norm), _inproj_weight(w_in))
    sb_gain = vec(sb_out_norm)
    dn = (dn_conv_w.astype(F32), lane(dn_a_log), lane(dn_dt_bias), vec(jnp.tile(dn_out_norm, (1, N_HEADS))))
    s5_mats = jax.vmap(_s5_matrices)(s5_a_re, s5_a_im, s5_log_dt, s5_b_re, s5_b_im, s5_c_re, s5_c_im, s5_d)
    tail = (bf(s5_w_glu), vec(s5_b_glu), vec(s5_out_norm), bf(w_out),
            vec(ffn2_norm), bf(ffn2_w_gate), bf(ffn2_w_up), bf(ffn2_w_down))
    for l in range(depth):
        h, sbq, sbk, sbv, dqkv, dz, dba, u5 = _ffn_inproj(h, l, *ffn1)
        o_sb = _sb_attention(tok3(sbq), tok3(sbk), tok3(sbv), l, sb_gain)
        o_dn = _deltanet(tok3(dqkv), tok3(dz), tok3(dba), l, *dn)
        y5 = _s5(tok3(u5), l, s5_mats)
        branches = (o_sb.reshape(ntok, HW), o_dn.reshape(ntok, HW), y5.reshape(ntok, -1))
        if l < depth - 1:
            h = _mixout_ffn(h, *branches, l, *tail)
    out = _mixout_ffn_final(h, *branches, depth - 1, *tail, final_norm[None], bsz, FRONT_PAD + N_META)
    return out.reshape(bsz, seq, d)
```

```python
import functools

import jax
import jax.numpy as jnp
from jax import lax
from jax.experimental import pallas as pl
from jax.experimental.pallas import tpu as pltpu

F32 = jnp.float32
BF16 = jnp.bfloat16

N_META = 16
HEAD_DIM = 64
N_HEADS = 4
HW = N_HEADS * HEAD_DIM
SB_BLOCK = 128
SB_WINDOW = 256
DN_CHUNK = 64
DN_CONV = 4
FF_CHUNK = 256
LANES = 128
S5_GROUP = 16
S5_STATE = 64
S5_T = 16
S5_LG = LANES // S5_GROUP
EPS = 1e-6
FRONT_PAD = (-N_META) % SB_BLOCK
EXP_UNDERFLOW = -88.0
VMEM_LIMIT = 56 * 1024 * 1024


def _cparams(sem):
    return pltpu.CompilerParams(dimension_semantics=sem, vmem_limit_bytes=VMEM_LIMIT)


def _pick_tile(n, candidates):
    for c in candidates:
        if n % c == 0:
            return c
    raise ValueError(f"no tile for {n}")


def _rms(x):
    return x * lax.rsqrt(jnp.mean(x * x, axis=-1, keepdims=True) + EPS)


def _split_bf16(x):
    hi = x.astype(BF16)
    lo = (x - hi.astype(F32)).astype(BF16)
    return hi, lo


def _dot(a, b):
    return jnp.dot(a, b, preferred_element_type=F32)


def _dot_nt(a, b):
    return lax.dot_general(a, b, (((1,), (1,)), ((), ())), preferred_element_type=F32)


def _dot_tn(a, b):
    return lax.dot_general(a, b, (((0,), (0,)), ((), ())), preferred_element_type=F32)


def _dot_split(x, m_bf16):
    hi, lo = _split_bf16(x)
    return _dot(hi, m_bf16) + _dot(lo, m_bf16)


def _lockstep(gens):
    results = [None] * len(gens)
    live = list(range(len(gens)))
    while live:
        still = []
        for n in live:
            try:
                next(gens[n])
                still.append(n)
            except StopIteration as stop:
                results[n] = stop.value
        live = still
    return results


def _const_spec(shape):
    nd = len(shape)
    return pl.BlockSpec(shape, lambda *_: (0,) * nd, pipeline_mode=pl.Buffered(1))


def _layer_spec(a, l):
    nd = a.ndim - 1
    return pl.BlockSpec((None,) + a.shape[1:], lambda *_: (l,) + (0,) * nd, pipeline_mode=pl.Buffered(1))


def _swiglu_half_step(x, g_ref, wg_ref, wu_ref, wd_ref, acc_ref):
    xn = (_rms(x) * g_ref[...]).astype(BF16)

    def chunk_out(c):
        cols = pl.ds(pl.multiple_of(c * FF_CHUNK, FF_CHUNK), FF_CHUNK)
        gate = _dot(xn, wg_ref[:, cols])
        up = _dot(xn, wu_ref[:, cols])
        act = (gate * jax.nn.sigmoid(gate) * up).astype(BF16)
        return _dot(act, wd_ref[cols, :])

    acc_ref[...] = chunk_out(0)

    def chunk(c, carry):
        acc_ref[...] += chunk_out(c)
        return carry

    lax.fori_loop(1, wg_ref.shape[1] // FF_CHUNK, chunk, 0, unroll=True)
    return x + 0.5 * acc_ref[...]


_C_SBQ, _C_SBK, _C_SBV, _C_DNQKV, _C_DNZ, _C_DNBA, _C_S5U, _C_END = 0, 256, 512, 768, 1536, 1792, 1920, 2432
_PROJ_WIDTHS = (HW, HW, HW, 3 * HW, HW, LANES, 512)
_PROJ_DTYPES = (BF16, BF16, BF16, F32, F32, F32, F32)


def _ffn_inproj_body(h_ref, g_ref, wg_ref, wu_ref, wd_ref, gm_ref, w_ref,
                     o_ref, q_ref, k_ref, v_ref, dqkv_ref, dz_ref, dba_ref, u_ref, acc_ref):
    h = _swiglu_half_step(h_ref[...], g_ref, wg_ref, wu_ref, wd_ref, acc_ref)
    o_ref[...] = h
    xn = (_rms(h) * gm_ref[...]).astype(BF16)
    q_ref[...] = (_dot(xn, w_ref[:, _C_SBQ:_C_SBK]) * (HEAD_DIM ** -0.5)).astype(BF16)
    k_ref[...] = _dot(xn, w_ref[:, _C_SBK:_C_SBV]).astype(BF16)
    v_ref[...] = _dot(xn, w_ref[:, _C_SBV:_C_DNQKV]).astype(BF16)
    dqkv_ref[...] = _dot(xn, w_ref[:, _C_DNQKV:_C_DNZ])
    dz_ref[...] = _dot(xn, w_ref[:, _C_DNZ:_C_DNBA])
    dba_ref[...] = _dot(xn, w_ref[:, _C_DNBA:_C_S5U])
    u_ref[...] = _dot(xn, w_ref[:, _C_S5U:_C_END])


def _ffn_inproj(h, l, g, wg, wu, wd, gm, w):
    ntok, d = h.shape
    tm = _pick_tile(ntok, (768, 512, 384, 256, 128))
    row = lambda wd_: pl.BlockSpec((tm, wd_), lambda i: (i, 0))
    return pl.pallas_call(
        _ffn_inproj_body,
        grid=(ntok // tm,),
        in_specs=[row(d)] + [_layer_spec(a, l) for a in (g, wg, wu, wd, gm, w)],
        out_specs=[row(d)] + [row(wd_) for wd_ in _PROJ_WIDTHS],
        out_shape=[jax.ShapeDtypeStruct((ntok, d), F32)]
        + [jax.ShapeDtypeStruct((ntok, wd_), dt) for wd_, dt in zip(_PROJ_WIDTHS, _PROJ_DTYPES)],
        scratch_shapes=[pltpu.VMEM((tm, d), F32)],
        compiler_params=_cparams(("parallel",)),
        name="ffn_inproj",
    )(h, g, wg, wu, wd, gm, w)


def _mixout_ffn_body(h_ref, osb_ref, odn_ref, y_ref, wglu_ref, bglu_ref, g5_ref, wout_ref,
                     g_ref, wg_ref, wu_ref, wd_ref, *rest, final):
    if final:
        gf_ref, o_ref, acc_ref = rest
    else:
        o_ref, acc_ref = rest
    y = y_ref[...]
    gate = _dot(y.astype(BF16), wglu_ref[...]) + bglu_ref[...]
    o5 = (_rms(y * jax.nn.sigmoid(gate)) * g5_ref[...]).astype(BF16)
    mixed = (_dot(osb_ref[...], wout_ref[0:HW, :]) + _dot(odn_ref[...], wout_ref[HW:2 * HW, :])
             + _dot(o5, wout_ref[2 * HW:, :]))
    h = _swiglu_half_step(h_ref[...] + mixed, g_ref, wg_ref, wu_ref, wd_ref, acc_ref)
    o_ref[...] = _rms(h) * gf_ref[...] if final else h


def _mixout_ffn(h, osb, odn, y5, l, wglu, bglu, g5, wout, g, wg, wu, wd):
    ntok, d = h.shape
    tm = _pick_tile(ntok, (768, 512, 384, 256, 128))
    row = lambda w: pl.BlockSpec((tm, w), lambda i: (i, 0))
    consts = [wglu, bglu, g5, wout, g, wg, wu, wd]
    return pl.pallas_call(
        functools.partial(_mixout_ffn_body, final=False),
        grid=(ntok // tm,),
        in_specs=[row(d), row(HW), row(HW), row(512)] + [_layer_spec(c, l) for c in consts],
        out_specs=row(d),
        out_shape=jax.ShapeDtypeStruct((ntok, d), F32),
        scratch_shapes=[pltpu.VMEM((tm, d), F32)],
        compiler_params=_cparams(("parallel",)),
        name="mixout_ffn",
    )(h, osb, odn, y5, *consts)


def _mixout_ffn_final(h, osb, odn, y5, l, wglu, bglu, g5, wout, g, wg, wu, wd, gf, bsz, skip):
    ntok, d = h.shape
    lp = ntok // bsz
    seq = lp - skip
    tm = _pick_tile(seq, (1024, 512, 256, 128))
    per = seq // tm
    sub = 8
    assert lp % sub == 0 and skip % sub == 0
    row = lambda w: pl.BlockSpec(
        (pl.Element(tm), pl.Element(w)),
        lambda b, i: ((b * (lp // sub) + skip // sub + i * (tm // sub)) * sub, 0))
    consts = [wglu, bglu, g5, wout, g, wg, wu, wd]
    return pl.pallas_call(
        functools.partial(_mixout_ffn_body, final=True),
        grid=(bsz, per),
        in_specs=[row(d), row(HW), row(HW), row(512)] + [_layer_spec(c, l) for c in consts]
        + [_const_spec(gf.shape)],
        out_specs=pl.BlockSpec((tm, d), lambda b, i: (b * per + i, 0)),
        out_shape=jax.ShapeDtypeStruct((bsz * seq, d), F32),
        scratch_shapes=[pltpu.VMEM((tm, d), F32)],
        compiler_params=_cparams(("parallel", "parallel")),
        name="mixout_ffn_final",
    )(h, osb, odn, y5, *consts, gf)


def _sb_body(q_ref, k_ref, v_ref, sfx_ref, g_ref, o_ref):
    i = pl.program_id(0)
    nb = q_ref.shape[0]
    tq, tk = SB_BLOCK, SB_WINDOW
    row = lax.broadcasted_iota(jnp.int32, (tq, tk), 0)
    col = lax.broadcasted_iota(jnp.int32, (tq, tk), 1)
    qpos = i * tq + row
    chains = [(b, h) for b in range(nb) for h in range(N_HEADS)]
    qs = [q_ref[b, :, h * HEAD_DIM:(h + 1) * HEAD_DIM] for b, h in chains]
    sfx = sfx_ref[...]

    def cond(state):
        w0, done = state[0], state[1]
        return jnp.logical_and(w0 + tk > 0, jnp.logical_not(done))

    def body(state):
        w0, _, carry, acc = state
        start = pl.multiple_of(jnp.maximum(w0, 0), tq)
        kpos = start + col
        valid = jnp.logical_and(kpos < jnp.minimum(qpos, w0 + tk), kpos >= FRONT_PAD)

        def chain(n, b, h):
            hs = slice(h * HEAD_DIM, (h + 1) * HEAD_DIM)
            z = _dot_nt(qs[n], k_ref[b, pl.ds(start, tk), hs])
            yield
            sp = jnp.maximum(z, 0.0) + jnp.log(1.0 + jnp.exp(-jnp.abs(z)))
            lk = jnp.where(valid, -sp, 0.0)
            sums = _dot_split(lk, sfx)
            yield
            c = carry[n]
            expo = z - sp + sums[:, 0:tk] + jnp.concatenate([c] * (tk // LANES), axis=1)
            w = jnp.where(valid, jnp.exp(expo), 0.0)
            pv = _dot(w.astype(BF16), v_ref[b, pl.ds(start, tk), hs])
            yield
            return c + sums[:, tk:], acc[n] + pv

        results = _lockstep([chain(n, b, h) for n, (b, h) in enumerate(chains)])
        new_carry = tuple(r[0] for r in results)
        top = functools.reduce(jnp.maximum, new_carry)
        return w0 - tk, jnp.max(top) < EXP_UNDERFLOW, new_carry, tuple(r[1] for r in results)

    zc = tuple(jnp.zeros((tq, LANES), F32) for _ in chains)
    za = tuple(jnp.zeros((tq, HEAD_DIM), F32) for _ in chains)
    _, _, _, acc = lax.while_loop(cond, body, ((i + 1) * tq - tk, jnp.bool_(False), zc, za))
    gain = g_ref[...]
    for b in range(nb):
        o_ref[b] = jnp.concatenate([_rms(acc[b * N_HEADS + h]) * gain for h in range(N_HEADS)],
                                   axis=1).astype(o_ref.dtype)


def _sb_attention(q, k, v, l, gain):
    bsz, lp, _ = q.shape
    nblk = lp // SB_BLOCK
    tk = SB_WINDOW
    r = jnp.arange(tk)
    sfx = jnp.concatenate([(r[:, None] > r[None, :]), jnp.ones((tk, LANES), bool)], axis=1).astype(BF16)
    full = pl.BlockSpec((bsz, lp, HW), lambda i: (0, 0, 0), pipeline_mode=pl.Buffered(1))
    return pl.pallas_call(
        _sb_body,
        grid=(nblk,),
        in_specs=[pl.BlockSpec((bsz, SB_BLOCK, HW), lambda i: (0, i, 0)), full, full,
                  _const_spec(sfx.shape), _layer_spec(gain, l)],
        out_specs=pl.BlockSpec((bsz, SB_BLOCK, HW), lambda i: (0, i, 0)),
        out_shape=jax.ShapeDtypeStruct((bsz, lp, HW), BF16),
        compiler_params=_cparams(("parallel",)),
        name="sb_attention",
    )(q, k, v, sfx, gain)


def _head_mask(rows, cols, rdiv, cdiv):
    r = lax.broadcasted_iota(jnp.int32, (rows, cols), 0) // rdiv
    c = lax.broadcasted_iota(jnp.int32, (rows, cols), 1) // cdiv
    return r == c


def _dn_body(qkv_ref, halo_ref, z_ref, ba_ref, cw_ref, alog_ref, dtb_ref, gain_ref, o_ref,
             q_s, k_s, kb_s, vb_s, gs_s, gc_s, u_s, wq_s, a_s, kg_s, o_s, state_s, *, tb):
    i = pl.program_id(0)
    nb = qkv_ref.shape[0]
    c = DN_CHUNK

    @pl.when(i == 0)
    def _():
        state_s[...] = jnp.zeros_like(state_s)

    bd_mask = _head_mask(HW, HW, HEAD_DIM, HEAD_DIM)
    ones_bd = jnp.where(bd_mask, 1.0, 0.0).astype(BF16)
    rowi = lax.broadcasted_iota(jnp.int32, (c, HW), 0)
    colj = lax.broadcasted_iota(jnp.int32, (c, HW), 1) % HEAD_DIM
    strict = rowi > colj
    incl = rowi >= colj
    eye = jnp.where(rowi == colj, 1.0, 0.0)
    strict3 = jnp.concatenate([strict] * 3, axis=1)
    tri = jnp.where(lax.broadcasted_iota(jnp.int32, (c, c), 0) >= lax.broadcasted_iota(jnp.int32, (c, c), 1),
                    1.0, 0.0).astype(BF16)
    rt = lax.broadcasted_iota(jnp.int32, (tb, tb), 0)
    ct = lax.broadcasted_iota(jnp.int32, (tb, tb), 1)
    tri_blk = jnp.where(jnp.logical_and(rt // c == ct // c, rt >= ct), 1.0, 0.0).astype(BF16)

    def headsum(x):
        return _dot_split(x, ones_bd)

    def blockdiag(y):
        return jnp.where(bd_mask, jnp.concatenate([y] * N_HEADS, axis=0), jnp.zeros((), y.dtype))

    def lanes_of(x, first):
        return jnp.concatenate(
            [jnp.broadcast_to(x[:, first + h:first + h + 1], (tb, HEAD_DIM)) for h in range(N_HEADS)], axis=1)

    def sum3(x):
        return x[:, 0:HW] + x[:, HW:2 * HW] + x[:, 2 * HW:3 * HW]

    cw = cw_ref[...]
    real = (i * tb + lax.broadcasted_iota(jnp.int32, (tb, 1), 0)) >= FRONT_PAD
    for b in range(nb):
        x = qkv_ref[b]
        halo = jnp.where(i > 0, halo_ref[b], 0.0)
        xx = jnp.concatenate([halo, x], axis=0)
        conv = (cw[3:4] * x + cw[2:3] * xx[7:7 + tb] + cw[1:2] * xx[6:6 + tb] + cw[0:1] * xx[5:5 + tb])
        y = conv * jax.nn.sigmoid(conv)
        qr, kr, v = y[:, 0:HW], y[:, HW:2 * HW], y[:, 2 * HW:3 * HW]
        q = qr * lax.rsqrt(headsum(qr * qr) + EPS) * (HEAD_DIM ** -0.5)
        k = kr * lax.rsqrt(headsum(kr * kr) + EPS)
        ba = ba_ref[b]
        beta = jnp.where(real, jax.nn.sigmoid(ba), 0.0)
        gate = jnp.where(real, -jnp.exp(alog_ref[...]) * jax.nn.softplus(ba + dtb_ref[...]), 0.0)
        betab = lanes_of(beta, 0)
        gb = lanes_of(gate, N_HEADS)
        g1 = gb.astype(BF16)
        r1 = gb - g1.astype(F32)
        g2 = r1.astype(BF16)
        g3 = (r1 - g2.astype(F32)).astype(BF16)
        gs = jnp.concatenate([g1, g2, g3], axis=1)
        q_s[b] = q
        k_s[b] = k
        kb_s[b] = k * betab
        vb_s[b] = v * betab
        gs_s[b] = gs
        gc_s[b] = sum3(_dot(tri_blk, gs))

    def chunk_matrices(q, k, kb, vb, gc, gs):
        dm = sum3(_dot(tri, jnp.where(strict3, gs, jnp.zeros((), BF16))))
        decay = jnp.exp(dm)
        kq = _dot_nt(jnp.concatenate([kb, q], axis=0).astype(BF16), blockdiag(k.astype(BF16)))
        yield
        lmat = jnp.where(strict, kq[0:c] * decay, 0.0)
        attn = jnp.where(incl, kq[c:] * decay, 0.0)
        neg = -lmat
        ph, plo = _split_bf16(neg)
        res = _dot(jnp.concatenate([ph, plo], axis=0), blockdiag(ph))
        res2 = _dot(ph, blockdiag(plo))
        yield
        power = res[0:c] + res[c:] + res2
        tinv = eye + neg
        for step in range(1, 6):
            ph, plo = _split_bf16(power)
            th, tlo = _split_bf16(tinv)
            bh, bl = blockdiag(ph), blockdiag(plo)
            if step < 5:
                res = _dot(jnp.concatenate([ph, plo, th, tlo], axis=0), bh)
                res2 = _dot(jnp.concatenate([ph, th], axis=0), bl)
                yield
                power = res[0:c] + res[c:2 * c] + res2[0:c]
                tinv = tinv + (res[2 * c:3 * c] + res[3 * c:] + res2[c:])
            else:
                res = _dot(jnp.concatenate([th, tlo], axis=0), bh)
                res2 = _dot(th, bl)
                yield
                tinv = tinv + (res[0:c] + res[c:] + res2)
        egc = jnp.exp(gc)
        uw = _dot(tinv.astype(BF16),
                  jnp.concatenate([blockdiag(vb.astype(BF16)), blockdiag((kb * egc).astype(BF16))], axis=1))
        return (uw[:, 0:HW], jnp.concatenate([uw[:, HW:], q * egc], axis=0).astype(BF16), attn.astype(BF16),
                (k * jnp.exp(gc[c - 1:c, :] - gc)).astype(BF16))

    per_trip = 2 if (tb // c) % 2 == 0 else 1

    def matrices(ti, carry):
        chains = [(b, ti * per_trip + n) for n in range(per_trip) for b in range(nb)]
        rows = [pl.ds(pl.multiple_of(ci * c, c), c) for _, ci in chains]
        loaded = [tuple(s[b, r, :] for s in (q_s, k_s, kb_s, vb_s, gc_s, gs_s)) for (b, _), r in zip(chains, rows)]
        results = _lockstep([chunk_matrices(*vals) for vals in loaded])
        for (b, ci), r, (u, wq, attn, kg) in zip(chains, rows, results):
            u_s[b, r, :] = u
            wq_s[b, pl.ds(pl.multiple_of(ci * 2 * c, 2 * c), 2 * c), :] = wq
            a_s[b, r, :] = attn
            kg_s[b, r, :] = kg
        return carry

    lax.fori_loop(0, tb // (c * per_trip), matrices, 0)

    def recur(ci, carry):
        r = pl.ds(pl.multiple_of(ci * c, c), c)
        r2 = pl.ds(pl.multiple_of(ci * 2 * c, 2 * c), 2 * c)
        loaded = [(state_s[b], wq_s[b, r2, :], u_s[b, r, :], a_s[b, r, :], kg_s[b, r, :],
                   gc_s[b, pl.ds(ci * c + c - 1, 1), :]) for b in range(nb)]

        def chunk_step(state, lhs, u, attn, kg, g_last):
            wq = _dot(lhs, state.astype(BF16))
            yield
            v_new = (u - wq[0:c]).astype(BF16)
            av = _dot(attn, blockdiag(v_new))
            upd = _dot_tn(kg, v_new)
            yield
            return wq[c:] + av, state * jnp.exp(g_last) + jnp.where(bd_mask, upd, 0.0)

        results = _lockstep([chunk_step(*vals) for vals in loaded])
        for b, (o, state) in enumerate(results):
            o_s[b, r, :] = o
            state_s[b] = state
        return carry

    lax.fori_loop(0, tb // c, recur, 0, unroll=2)

    gain = gain_ref[...]
    for b in range(nb):
        o = o_s[b]
        zz = z_ref[b]
        on = o * lax.rsqrt(headsum(o * o) * (1.0 / HEAD_DIM) + EPS) * gain
        o_ref[b] = (on * (zz * jax.nn.sigmoid(zz))).astype(o_ref.dtype)


def _deltanet(qkv, z, ba, l, conv_w, alog, dtb, gain):
    bsz, lp, _ = qkv.shape
    tb = _pick_tile(lp, (384, 128))
    halo_blocks = tb // 8
    row = lambda w: pl.BlockSpec((bsz, tb, w), lambda i: (0, i, 0))
    f32s = pltpu.VMEM((bsz, tb, HW), F32)
    b16s = pltpu.VMEM((bsz, tb, HW), BF16)
    return pl.pallas_call(
        functools.partial(_dn_body, tb=tb),
        grid=(lp // tb,),
        in_specs=[row(3 * HW),
                  pl.BlockSpec((bsz, 8, 3 * HW), lambda i: (0, jnp.maximum(i * halo_blocks - 1, 0), 0)),
                  row(HW), row(LANES),
                  _layer_spec(conv_w, l), _layer_spec(alog, l), _layer_spec(dtb, l), _layer_spec(gain, l)],
        out_specs=row(HW),
        out_shape=jax.ShapeDtypeStruct((bsz, lp, HW), BF16),
        scratch_shapes=[f32s, f32s, f32s, f32s,
                        pltpu.VMEM((bsz, tb, 3 * HW), BF16), f32s,
                        f32s, pltpu.VMEM((bsz, 2 * tb, HW), BF16), b16s, b16s,
                        f32s, pltpu.VMEM((bsz, HW, HW), F32)],
        compiler_params=_cparams(("arbitrary",)),
        name="deltanet",
    )(qkv, qkv, z, ba, conv_w, alog, dtb, gain)


def _s5_body(u_ref, mg_ref, wg_ref, vg_ref, ar_ref, ai_ref, d_ref, y_ref, m_ref, w_ref, v_ref, e_s, sp_s, st_s):
    n = u_ref.shape[0] // S5_T
    half = S5_LG * S5_STATE
    t_, cg, p = S5_T, S5_GROUP, S5_STATE

    @pl.when(jnp.logical_and(pl.program_id(1) == 0, pl.program_id(2) == 0))
    def _():
        def placement(rows, cols, col_key, col_group, r):
            rr = lax.broadcasted_iota(jnp.int32, (rows, cols), 0)
            cc = lax.broadcasted_iota(jnp.int32, (rows, cols), 1)
            return jnp.where(jnp.logical_and(rr == col_key(cc), col_group(cc) == r), 1.0, 0.0).astype(BF16)

        for r in range(S5_LG):
            p_out = placement(t_ * cg, t_ * LANES, lambda c: (c // LANES) * cg + c % cg, lambda c: (c // cg) % S5_LG, r)
            p_state = placement(2 * p, 2 * half, lambda c: (c // half) * p + c % p, lambda c: (c // p) % S5_LG, r)
            mr = _dot(mg_ref[r], p_out).astype(BF16)
            wr = _dot(wg_ref[r], p_state).astype(BF16)
            vr = _dot(vg_ref[r], p_out).astype(BF16)
            for tau in range(t_):
                dst = slice(tau * LANES + r * cg, tau * LANES + (r + 1) * cg)
                m_ref[dst, :] = mr[tau * cg:(tau + 1) * cg, :]
                w_ref[dst, :] = wr[tau * cg:(tau + 1) * cg, :]
            for ri in range(2):
                v_ref[ri * half + r * p:ri * half + (r + 1) * p, :] = vr[ri * p:(ri + 1) * p, :]

    @pl.when(pl.program_id(2) == 0)
    def _():
        st_s[...] = jnp.zeros_like(st_s)

    xs = [u_ref[pl.ds(t, n, stride=S5_T), :] for t in range(S5_T)]
    xcat = jnp.concatenate(xs, axis=1).astype(BF16)
    e_s[...] = _dot(xcat, w_ref[...])
    ar = ar_ref[...]
    ai = ai_ref[...]

    def step(ci, s):
        sr, si = s
        row = pl.ds(ci, 1)
        sp_s[row, 0:half] = sr
        sp_s[row, half:] = si
        er = e_s[row, 0:half]
        ei = e_s[row, half:]
        return ar * sr - ai * si + er, ar * si + ai * sr + ei

    sr, si = lax.fori_loop(0, n, step, (st_s[0:1, :], st_s[1:2, :]), unroll=4)
    st_s[0:1, :] = sr
    st_s[1:2, :] = si
    sp = sp_s[...].astype(BF16)
    d = d_ref[...]
    cb = 2 * LANES
    for k in range(S5_T * LANES // cb):
        cols = slice(k * cb, (k + 1) * cb)
        yk = _dot(xcat[:, 0:(k + 1) * cb], m_ref[0:(k + 1) * cb, cols]) + _dot(sp, v_ref[:, cols])
        for t in range(2 * k, 2 * k + 2):
            off = (t - 2 * k) * LANES
            y_ref[pl.ds(t, n, stride=S5_T), :] = jax.nn.gelu(yk[:, off:off + LANES] + d * xs[t])


def _s5(u, l, mats):
    mg, wg, vg, ar, ai, d = mats
    bsz, lp, width = u.shape
    ntile = width // LANES
    rows = _pick_tile(lp, (lp // 2,))
    n = rows // S5_T
    half = S5_LG * S5_STATE
    mat = lambda a: pl.BlockSpec((None, None) + a.shape[2:], lambda j, b, r: (l, j) + (0,) * (a.ndim - 2))
    return pl.pallas_call(
        _s5_body,
        grid=(ntile, bsz, lp // rows),
        in_specs=[pl.BlockSpec((None, rows, LANES), lambda j, b, r: (b, r, j)),
                  mat(mg), mat(wg), mat(vg), mat(ar), mat(ai), mat(d)],
        out_specs=pl.BlockSpec((None, rows, LANES), lambda j, b, r: (b, r, j)),
        out_shape=jax.ShapeDtypeStruct(u.shape, F32),
        scratch_shapes=[pltpu.VMEM((S5_T * LANES, S5_T * LANES), BF16), pltpu.VMEM((S5_T * LANES, 2 * half), BF16),
                        pltpu.VMEM((2 * half, S5_T * LANES), BF16),
                        pltpu.VMEM((n, 2 * half), F32), pltpu.VMEM((n, 2 * half), F32), pltpu.VMEM((8, half), F32)],
        compiler_params=_cparams(("arbitrary", "arbitrary", "arbitrary")),
        name="s5",
    )(u, mg, wg, vg, ar, ai, d)


def _s5_matrices(a_re, a_im, log_dt, b_re, b_im, c_re, c_im, d):
    ng, p = a_re.shape
    t, lg, cg = S5_T, S5_LG, S5_GROUP
    nj = ng // lg
    dt = jnp.exp(log_dt)[:, None]
    lr, li = a_re * dt, a_im * dt
    er = jnp.exp(lr)
    abr, abi = er * jnp.cos(li), er * jnp.sin(li)
    den = a_re * a_re + a_im * a_im
    fr = ((abr - 1.0) * a_re + abi * a_im) / den
    fi = (abi * a_re - (abr - 1.0) * a_im) / den
    bbr = fr[..., None] * b_re - fi[..., None] * b_im
    bbi = fr[..., None] * b_im + fi[..., None] * b_re
    n = jnp.arange(t + 1, dtype=F32)[:, None, None]
    pr = jnp.exp(n * lr[None]) * jnp.cos(n * li[None])
    pi = jnp.exp(n * lr[None]) * jnp.sin(n * li[None])
    car = c_re[None] * pr[:, :, None, :] - c_im[None] * pi[:, :, None, :]
    cai = c_re[None] * pi[:, :, None, :] + c_im[None] * pr[:, :, None, :]
    kern = jnp.einsum('ngop,gpi->ngio', car[:t], bbr) - jnp.einsum('ngop,gpi->ngio', cai[:t], bbi)
    rows = lax.broadcasted_iota(jnp.int32, (t * cg, t * t * cg), 0)
    cols = lax.broadcasted_iota(jnp.int32, (t * cg, t * t * cg), 1)
    toeplitz = jnp.logical_and((cols // cg) % t - cols // (t * cg) == rows // cg, cols % cg == rows % cg)
    k3 = jnp.transpose(kern, (1, 2, 0, 3)).reshape(ng * cg, t * cg).astype(BF16)
    mg = jnp.dot(k3, toeplitz.astype(BF16), preferred_element_type=BF16)
    mg = jnp.transpose(mg.reshape(ng, cg, t, t * cg), (0, 2, 1, 3)).reshape(nj, lg, t * cg, t * cg)
    nrev = (t - 1) - jnp.arange(t, dtype=F32)[:, None, None]
    rev_r = (jnp.exp(nrev * lr[None]) * jnp.cos(nrev * li[None]))[..., None]
    rev_i = (jnp.exp(nrev * lr[None]) * jnp.sin(nrev * li[None]))[..., None]
    wg = jnp.stack([rev_r * bbr[None] - rev_i * bbi[None], rev_r * bbi[None] + rev_i * bbr[None]])
    wg = jnp.transpose(wg, (2, 1, 4, 0, 3)).reshape(nj, lg, t * cg, 2 * p).astype(BF16)
    vg = jnp.stack([car[1:], -cai[1:]])
    vg = jnp.transpose(vg, (2, 0, 4, 1, 3)).reshape(nj, lg, 2 * p, t * cg).astype(BF16)

    ar = pr[t].reshape(nj, 1, lg * p)
    ai = pi[t].reshape(nj, 1, lg * p)
    return mg, wg, vg, ar, ai, d.reshape(nj, 1, LANES)


def _inproj_weight(w_in):
    pad = jnp.zeros(w_in.shape[:-1] + (LANES - 2 * N_HEADS,), w_in.dtype)
    return jnp.concatenate([w_in[..., 0:1800], pad, w_in[..., 1800:2312]], axis=-1).astype(BF16)


def kernel(x, meta_tokens, ffn1_norm, ffn1_w_gate, ffn1_w_up, ffn1_w_down, mix_norm, w_in, sb_out_norm, dn_conv_w, dn_a_log, dn_dt_bias, dn_out_norm, s5_a_re, s5_a_im, s5_log_dt, s5_b_re, s5_b_im, s5_c_re, s5_c_im, s5_d, s5_w_glu, s5_b_glu, s5_out_norm, w_out, ffn2_norm, ffn2_w_gate, ffn2_w_up, ffn2_w_down, final_norm):
    bsz, seq, d = x.shape
    depth = w_in.shape[0]
    lp = FRONT_PAD + N_META + seq
    ntok = bsz * lp
    assert lp % SB_BLOCK == 0 and lp % DN_CHUNK == 0 and lp % (2 * S5_T) == 0
    meta = jnp.broadcast_to(meta_tokens[None].astype(x.dtype), (bsz, N_META, d))
    h = jnp.concatenate([jnp.zeros((bsz, FRONT_PAD, d), x.dtype), meta, x], axis=1).reshape(ntok, d)
    tok3 = lambda a: a.reshape(bsz, lp, a.shape[-1])
    bf = lambda a: a.astype(BF16)
    vec = lambda a: a.astype(F32)[:, None, :]
    lane = lambda a: jnp.zeros((depth, 1, LANES), F32).at[:, 0, N_HEADS:2 * N_HEADS].set(a.astype(F32))
    ffn1 = (vec(ffn1_norm), bf(ffn1_w_gate), bf(ffn1_w_up), bf(ffn1_w_down), vec(mix_norm), _inproj_weight(w_in))
    sb_gain = vec(sb_out_norm)
    dn = (dn_conv_w.astype(F32), lane(dn_a_log), lane(dn_dt_bias), vec(jnp.tile(dn_out_norm, (1, N_HEADS))))
    s5_mats = jax.vmap(_s5_matrices)(s5_a_re, s5_a_im, s5_log_dt, s5_b_re, s5_b_im, s5_c_re, s5_c_im, s5_d)
    tail = (bf(s5_w_glu), vec(s5_b_glu), vec(s5_out_norm), bf(w_out),
            vec(ffn2_norm), bf(ffn2_w_gate), bf(ffn2_w_up), bf(ffn2_w_down))
    for l in range(depth):
        h, sbq, sbk, sbv, dqkv, dz, dba, u5 = _ffn_inproj(h, l, *ffn1)
        o_sb = _sb_attention(tok3(sbq), tok3(sbk), tok3(sbv), l, sb_gain)
        o_dn = _deltanet(tok3(dqkv), tok3(dz), tok3(dba), l, *dn)
        y5 = _s5(tok3(u5), l, s5_mats)
        branches = (o_sb.reshape(ntok, HW), o_dn.reshape(ntok, HW), y5.reshape(ntok, -1))
        if l < depth - 1:
            h = _mixout_ffn(h, *branches, l, *tail)
    out = _mixout_ffn_final(h, *branches, depth - 1, *tail, final_norm[None], bsz, FRONT_PAD + N_META)
    return out.reshape(bsz, seq, d)
```
